```python
import math
import jax, jax.numpy as jnp
from jax import lax
import numpy as np

D_MODEL = 1024
BATCH = 8
SEQ = 4096
DEPTH = 4

N_CONV_LAYERS = DEPTH // 2
N_ATTN_LAYERS = DEPTH - N_CONV_LAYERS
CONV_WIDTH = 31
HEAD_DIM = 64
N_Q_HEADS = D_MODEL // HEAD_DIM
N_KV_HEADS = 4
GQA_GROUP = N_Q_HEADS // N_KV_HEADS
WINDOW = 128
ROPE_THETA = 10000.0
N_EXPERTS = 16
N_GROUPS = 4
EXPERTS_PER_GROUP = N_EXPERTS // N_GROUPS
TOP_K = 2
D_EXPERT = D_MODEL // 2
EXPERT_BLOCK = 256
LN_EPS = 1e-5
NEG_INF = -1e30
DEEPNORM_ALPHA = (2.0 * DEPTH) ** 0.25
DEEPNORM_BETA = (8.0 * DEPTH) ** -0.25

kernel_name = "yoco_conformer_swa_sink_grouped_moe"


def layer_norm(x, g, b):
    xf = x.astype(jnp.float32)
    mu = xf.mean(-1, keepdims=True)
    var = jnp.square(xf - mu).mean(-1, keepdims=True)
    y = (xf - mu) * lax.rsqrt(var + LN_EPS) * g.astype(jnp.float32) + b.astype(jnp.float32)
    return y.astype(x.dtype)


def deepnorm(x, sub, g, b):
    return layer_norm(DEEPNORM_ALPHA * x + sub, g, b)


def rope(x, positions):
    inv_freq = ROPE_THETA ** (-jnp.arange(0, HEAD_DIM, 2, dtype=jnp.float32) / HEAD_DIM)
    ang = positions.astype(jnp.float32)[..., None] * inv_freq
    cos = jnp.cos(ang)[:, :, None, :]
    sin = jnp.sin(ang)[:, :, None, :]
    xf = x.astype(jnp.float32)
    x1, x2 = xf[..., :HEAD_DIM // 2], xf[..., HEAD_DIM // 2:]
    out = jnp.concatenate([x1 * cos - x2 * sin, x2 * cos + x1 * sin], axis=-1)
    return out.astype(x.dtype)


def conformer_conv(x, w1, b1, dw, dwb, g, bn, w2, b2):
    h = x @ w1 + b1
    a, gate = jnp.split(h, 2, axis=-1)
    h = a * jax.nn.sigmoid(gate)
    h = lax.conv_general_dilated(
        h, dw[:, None, :].astype(h.dtype), window_strides=(1,),
        padding=[(CONV_WIDTH - 1, 0)], dimension_numbers=('NWC', 'WIO', 'NWC'),
        feature_group_count=D_MODEL) + dwb
    h = jax.nn.silu(layer_norm(h, g, bn))
    return h @ w2 + b2


def shared_kv(x, positions, kv_w, kv_b):
    B, S, _ = x.shape
    kv = x @ kv_w + kv_b
    k, v = jnp.split(kv, 2, axis=-1)
    k = rope(k.reshape(B, S, N_KV_HEADS, HEAD_DIM), positions)
    v = v.reshape(B, S, N_KV_HEADS, HEAD_DIM)
    return k, v


def window_mask(nb):
    n = jnp.arange(nb)[:, None, None]
    i = jnp.arange(WINDOW)[None, :, None]
    j = jnp.arange(2 * WINDOW)[None, None, :]
    qpos = n * WINDOW + i
    kpos = (n - 1) * WINDOW + j
    d = qpos - kpos
    return (d >= 0) & (d < WINDOW) & (kpos >= 0)


def swa_sink_attention(x, positions, wq, bq, sinks, k, v, wo, bo):
    B, S, _ = x.shape
    nb = S // WINDOW
    q = rope((x @ wq + bq).reshape(B, S, N_Q_HEADS, HEAD_DIM), positions)
    qb = q.reshape(B, nb, WINDOW, N_KV_HEADS, GQA_GROUP, HEAD_DIM)
    kb = k.reshape(B, nb, WINDOW, N_KV_HEADS, HEAD_DIM)
    vb = v.reshape(B, nb, WINDOW, N_KV_HEADS, HEAD_DIM)
    pad = ((0, 0), (1, 0), (0, 0), (0, 0), (0, 0))
    k2 = jnp.concatenate([jnp.pad(kb[:, :-1], pad), kb], axis=2)
    v2 = jnp.concatenate([jnp.pad(vb[:, :-1], pad), vb], axis=2)
    s = jnp.einsum('bnqkgd,bnskd->bnkgqs', qb, k2,
                   preferred_element_type=jnp.float32) * (1.0 / math.sqrt(HEAD_DIM))
    s = jnp.where(window_mask(nb)[None, :, None, None], s, NEG_INF)
    sink = sinks.astype(jnp.float32).reshape(N_KV_HEADS, GQA_GROUP)[None, None, :, :, None, None]
    m = jnp.maximum(s.max(-1, keepdims=True), sink)
    p = jnp.exp(s - m)
    denom = p.sum(-1, keepdims=True) + jnp.exp(sink - m)
    o = jnp.einsum('bnkgqs,bnskd->bnqkgd', (p / denom).astype(v2.dtype), v2)
    o = o.reshape(B, S, D_MODEL)
    return o @ wo + bo


def grouped_moe(x, router_w, router_b, w_gate, w_up, w_down):
    B, S, D = x.shape
    xt = x.reshape(-1, D)
    N = xt.shape[0]
    aff = jax.nn.sigmoid(xt.astype(jnp.float32) @ router_w.astype(jnp.float32))
    sel = aff + router_b.astype(jnp.float32)
    grp = sel.reshape(N, N_GROUPS, EXPERTS_PER_GROUP)
    gscore = lax.top_k(grp, 2)[0].sum(-1)
    g = jnp.argmax(gscore, axis=-1)
    sel_in = jnp.take_along_axis(grp, g[:, None, None], axis=1)[:, 0, :]
    _, idx = lax.top_k(sel_in, TOP_K)
    expert = g[:, None] * EXPERTS_PER_GROUP + idx
    gate = jnp.take_along_axis(aff, expert, axis=1)
    gate = gate / gate.sum(-1, keepdims=True)

    A = N * TOP_K
    e_flat = expert.reshape(A)
    tok_flat = jnp.repeat(jnp.arange(N, dtype=jnp.int32), TOP_K)
    gate_flat = gate.reshape(A)
    order = jnp.argsort(e_flat)
    e_sorted = e_flat[order]
    counts = jnp.bincount(e_flat, length=N_EXPERTS)
    start = jnp.cumsum(counts) - counts
    rank = jnp.arange(A) - start[e_sorted]
    padded = ((counts + EXPERT_BLOCK - 1) // EXPERT_BLOCK) * EXPERT_BLOCK
    pend = jnp.cumsum(padded)
    pstart = pend - padded
    dest = pstart[e_sorted] + rank
    P = ((A + N_EXPERTS * EXPERT_BLOCK + EXPERT_BLOCK - 1) // EXPERT_BLOCK) * EXPERT_BLOCK
    tok_buf = jnp.zeros((P,), jnp.int32).at[dest].set(tok_flat[order])
    gate_buf = jnp.zeros((P,), jnp.float32).at[dest].set(gate_flat[order])
    nblk = P // EXPERT_BLOCK
    blk_start = jnp.arange(nblk) * EXPERT_BLOCK
    blk_expert = jnp.minimum(jnp.searchsorted(pend, blk_start, side='right'), N_EXPERTS - 1)
    xb = xt[tok_buf].reshape(nblk, EXPERT_BLOCK, D)

    def expert_block(args):
        xblk, e = args
        h = jax.nn.silu(xblk @ w_gate[e]) * (xblk @ w_up[e])
        return h @ w_down[e]

    yb = lax.map(expert_block, (xb, blk_expert)).reshape(P, D)
    y = jnp.zeros_like(xt).at[tok_buf].add(yb * gate_buf[:, None].astype(yb.dtype))
    return y.reshape(B, S, D)


def setup_inputs(seed: int = 0) -> dict:
    key = jax.random.key(seed)
    ks = iter(jax.random.split(key, 32))
    f32 = jnp.float32

    def nrm(shape, scale):
        return jax.random.normal(next(ks), shape, f32) * scale

    D, E, F = D_MODEL, N_EXPERTS, D_EXPERT
    nA, nB = N_CONV_LAYERS, N_ATTN_LAYERS
    kvd = N_KV_HEADS * HEAD_DIM
    inp = {}
    inp['x'] = jax.random.normal(next(ks), (BATCH, SEQ, D), f32)
    inp['positions'] = jnp.broadcast_to(jnp.arange(SEQ, dtype=jnp.int32), (BATCH, SEQ))
    inp['conv_w1'] = nrm((nA, D, 2 * D), D ** -0.5)
    inp['conv_b1'] = nrm((nA, 2 * D), 0.02)
    inp['conv_dw'] = nrm((nA, CONV_WIDTH, D), CONV_WIDTH ** -0.5)
    inp['conv_dwb'] = nrm((nA, D), 0.02)
    inp['conv_ln_g'] = 1.0 + nrm((nA, D), 0.02)
    inp['conv_ln_b'] = nrm((nA, D), 0.02)
    inp['conv_w2'] = nrm((nA, D, D), D ** -0.5 * DEEPNORM_BETA)
    inp['conv_b2'] = nrm((nA, D), 0.02)
    inp['kv_w'] = jnp.concatenate([nrm((D, kvd), D ** -0.5),
                                   nrm((D, kvd), D ** -0.5 * DEEPNORM_BETA)], axis=1)
    inp['kv_b'] = nrm((2 * kvd,), 0.02)
    inp['attn_wq'] = nrm((nB, D, N_Q_HEADS * HEAD_DIM), D ** -0.5)
    inp['attn_bq'] = nrm((nB, N_Q_HEADS * HEAD_DIM), 0.02)
    inp['attn_sinks'] = nrm((nB, N_Q_HEADS), 1.0)
    inp['attn_wo'] = nrm((nB, N_Q_HEADS * HEAD_DIM, D), D ** -0.5 * DEEPNORM_BETA)
    inp['attn_bo'] = nrm((nB, D), 0.02)
    inp['router_w'] = nrm((D, E), D ** -0.5)
    inp['router_b'] = nrm((E,), 0.01)
    inp['moe_w_gate'] = nrm((DEPTH, E, D, F), D ** -0.5)
    inp['moe_w_up'] = nrm((DEPTH, E, D, F), D ** -0.5)
    inp['moe_w_down'] = nrm((DEPTH, E, F, D), F ** -0.5 * DEEPNORM_BETA)
    inp['ln_g'] = 1.0 + nrm((DEPTH, 2, D), 0.02)
    inp['ln_b'] = nrm((DEPTH, 2, D), 0.02)
    return inp


def reference(x, positions, conv_w1, conv_b1, conv_dw, conv_dwb, conv_ln_g, conv_ln_b,
              conv_w2, conv_b2, kv_w, kv_b, attn_wq, attn_bq, attn_sinks, attn_wo, attn_bo,
              router_w, router_b, moe_w_gate, moe_w_up, moe_w_down, ln_g, ln_b):
    k_shared = None
    v_shared = None
    for layer in range(DEPTH):
        if layer < N_CONV_LAYERS:
            i = layer
            mix = conformer_conv(x, conv_w1[i], conv_b1[i], conv_dw[i], conv_dwb[i],
                                 conv_ln_g[i], conv_ln_b[i], conv_w2[i], conv_b2[i])
        else:
            if layer == N_CONV_LAYERS:
                k_shared, v_shared = shared_kv(x, positions, kv_w, kv_b)
            j = layer - N_CONV_LAYERS
            mix = swa_sink_attention(x, positions, attn_wq[j], attn_bq[j], attn_sinks[j],
                                     k_shared, v_shared, attn_wo[j], attn_bo[j])
        x = deepnorm(x, mix, ln_g[layer, 0], ln_b[layer, 0])
        ffn = grouped_moe(x, router_w, router_b, moe_w_gate[layer], moe_w_up[layer],
                          moe_w_down[layer])
        x = deepnorm(x, ffn, ln_g[layer, 1], ln_b[layer, 1])
    return x
```

```python
import functools
import math

import jax
import jax.numpy as jnp
from jax import lax
from jax.experimental import pallas as pl
from jax.experimental.pallas import tpu as pltpu

D_MODEL = 1024
BATCH = 8
SEQ = 4096
DEPTH = 4
N_TOK = BATCH * SEQ
N_CONV_LAYERS = DEPTH // 2
CONV_WIDTH = 31
HEAD_DIM = 64
N_Q_HEADS = D_MODEL // HEAD_DIM
N_KV_HEADS = 4
GQA_GROUP = N_Q_HEADS // N_KV_HEADS
KV_DIM = N_KV_HEADS * HEAD_DIM
WINDOW = 128
ROPE_THETA = 10000.0
N_EXPERTS = 16
N_GROUPS = 4
EXPERTS_PER_GROUP = N_EXPERTS // N_GROUPS
TOP_K = 2
D_EXPERT = D_MODEL // 2
EXPERT_BLOCK = 256
LN_EPS = 1e-5
NEG_INF = -1e30
DEEPNORM_ALPHA = (2.0 * DEPTH) ** 0.25

N_SLOTS = ((N_TOK * TOP_K + N_EXPERTS * EXPERT_BLOCK + EXPERT_BLOCK - 1)
           // EXPERT_BLOCK) * EXPERT_BLOCK
N_BLOCKS = N_SLOTS // EXPERT_BLOCK

LANES = 128
SUBLANES = 8
MIX_TILE = 512
CONV_HALO = 32
ROW_CHUNK = 64
MOVE_TILE = 512
KV_TILE = 1024
VMEM_LIMIT = 56 * 1024 * 1024

F32 = jnp.float32
BF16 = jnp.bfloat16


def _layer_norm(z, g, b):
    mu = jnp.mean(z, axis=-1, keepdims=True)
    zc = z - mu
    var = jnp.mean(zc * zc, axis=-1, keepdims=True)
    return zc * lax.rsqrt(var + LN_EPS) * g + b


def _top2_sum(a, b, c, d):
    hi1, lo1 = jnp.maximum(a, b), jnp.minimum(a, b)
    hi2, lo2 = jnp.maximum(c, d), jnp.minimum(c, d)
    top1 = jnp.maximum(hi1, hi2)
    top2 = jnp.maximum(jnp.minimum(hi1, hi2), jnp.maximum(lo1, lo2))
    return top1 + top2


def _argmax4(vals):
    best, idx = vals[0], jnp.zeros(vals[0].shape, jnp.int32)
    for j in range(1, 4):
        better = vals[j] > best
        idx = jnp.where(better, j, idx)
        best = jnp.where(better, vals[j], best)
    return idx, best


def _pick4(idx, vals):
    out = vals[3]
    for j in (2, 1, 0):
        out = jnp.where(idx == j, vals[j], out)
    return out


def _route(xb, rwt_ref, rb_ref, tri_ref, carry_ref, code_ref, gate_ref, cnt_ref, first):
    t = xb.shape[0]

    @pl.when(first)
    def _():
        carry_ref[...] = jnp.zeros_like(carry_ref)

    logits = lax.dot_general(rwt_ref[...], xb, (((1,), (1,)), ((), ())),
                             preferred_element_type=F32)
    aff = jax.nn.sigmoid(logits)
    sel = aff + rb_ref[...]
    sel_rows = [sel[e:e + 1, :] for e in range(N_EXPERTS)]
    aff_rows = [aff[e:e + 1, :] for e in range(N_EXPERTS)]
    gscore = [_top2_sum(*sel_rows[4 * g:4 * g + 4]) for g in range(N_GROUPS)]
    grp, _ = _argmax4(gscore)
    sel_in = [_pick4(grp, [sel_rows[4 * g + j] for g in range(N_GROUPS)]) for j in range(4)]
    aff_in = [_pick4(grp, [aff_rows[4 * g + j] for g in range(N_GROUPS)]) for j in range(4)]
    i0, _ = _argmax4(sel_in)
    masked = [jnp.where(i0 == j, -jnp.inf, sel_in[j]) for j in range(4)]
    i1, _ = _argmax4(masked)
    a0 = _pick4(i0, aff_in)
    a1 = _pick4(i1, aff_in)
    gsum = a0 + a1
    e0 = grp * EXPERTS_PER_GROUP + i0
    e1 = grp * EXPERTS_PER_GROUP + i1

    eiota = lax.broadcasted_iota(jnp.int32, (N_EXPERTS, t), 0)
    hit0 = eiota == e0
    hit1 = eiota == e1
    onehot = jnp.where(hit0 | hit1, 1.0, 0.0)
    before = jnp.dot(onehot.astype(BF16), tri_ref[...], preferred_element_type=F32)
    pos = before + carry_ref[:, 0:1]
    r0 = jnp.sum(jnp.where(hit0, pos, 0.0), axis=0, keepdims=True).astype(jnp.int32)
    r1 = jnp.sum(jnp.where(hit1, pos, 0.0), axis=0, keepdims=True).astype(jnp.int32)
    code_ref[0:1, :] = (e0 << 16) | r0
    code_ref[1:2, :] = (e1 << 16) | r1
    gate_ref[0:1, :] = a0 / gsum
    gate_ref[1:2, :] = a1 / gsum
    carry_ref[...] = carry_ref[...] + jnp.sum(onehot, axis=1, keepdims=True)
    cnt_ref[...] = carry_ref[...]


def _deepnorm_rows(x_ref, mix_ref, lg_ref, lb_ref, x1_ref, xb_ref):
    def body(i, c):
        rows = pl.ds(pl.multiple_of(i * ROW_CHUNK, ROW_CHUNK), ROW_CHUNK)
        z = DEEPNORM_ALPHA * x_ref[0, rows, :] + mix_ref[rows, :]
        y = _layer_norm(z, lg_ref[...], lb_ref[...])
        x1_ref[0, rows, :] = y
        xb_ref[rows, :] = y.astype(BF16)
        return c
    lax.fori_loop(0, MIX_TILE // ROW_CHUNK, body, 0)


def _conv_kernel(x_ref, xh_ref, w1_ref, b1_ref, dw_ref, dwb_ref, cg_ref, cb_ref, w2_ref, b2_ref,
                 lg_ref, lb_ref, rwt_ref, rb_ref, tri_ref,
                 x1_ref, code_ref, gate_ref, cnt_ref,
                 lhs_ref, h_ref, c_ref, act_ref, xb_ref, carry_ref):
    b = pl.program_id(0)
    s = pl.program_id(1)
    t = MIX_TILE

    lhs_ref[0:CONV_HALO, :] = xh_ref[0].astype(BF16)
    lhs_ref[CONV_HALO:, :] = x_ref[0].astype(BF16)
    lhs = lhs_ref[...]
    a = jnp.dot(lhs, w1_ref[:, :D_MODEL], preferred_element_type=F32) + b1_ref[:, :D_MODEL]
    gt = jnp.dot(lhs, w1_ref[:, D_MODEL:], preferred_element_type=F32) + b1_ref[:, D_MODEL:]
    h_ref[...] = a * jax.nn.sigmoid(gt)

    @pl.when(s == 0)
    def _():
        h_ref[0:CONV_HALO, :] = jnp.zeros((CONV_HALO, D_MODEL), F32)

    off0 = CONV_HALO - (CONV_WIDTH - 1)
    for c in range(D_MODEL // LANES):
        cols = slice(c * LANES, (c + 1) * LANES)

        def conv_body(i, carry, cols=cols):
            r0 = pl.multiple_of(i * ROW_CHUNK, ROW_CHUNK)
            win = h_ref[pl.ds(r0, ROW_CHUNK + CONV_HALO), cols]
            acc = jnp.zeros((ROW_CHUNK, LANES), F32)
            for r in range(SUBLANES):
                taps = [k for k in range(CONV_WIDTH) if (off0 + k) % SUBLANES == r]
                span = max(off0 + k for k in taps) - r + ROW_CHUNK
                shifted = win[r:r + span, :]
                for k in taps:
                    a0 = off0 + k - r
                    acc = acc + dw_ref[k:k + 1, cols] * shifted[a0:a0 + ROW_CHUNK, :]
            c_ref[pl.ds(r0, ROW_CHUNK), cols] = acc
            return carry
        lax.fori_loop(0, t // ROW_CHUNK, conv_body, 0)

    def act_body(i, carry):
        rows = pl.ds(pl.multiple_of(i * ROW_CHUNK, ROW_CHUNK), ROW_CHUNK)
        z = c_ref[rows, :] + dwb_ref[...]
        y = _layer_norm(z, cg_ref[...], cb_ref[...])
        act_ref[rows, :] = (y * jax.nn.sigmoid(y)).astype(BF16)
        return carry
    lax.fori_loop(0, t // ROW_CHUNK, act_body, 0)

    c_ref[...] = jnp.dot(act_ref[...], w2_ref[...], preferred_element_type=F32) + b2_ref[...]
    _deepnorm_rows(x_ref, c_ref, lg_ref, lb_ref, x1_ref, xb_ref)
    _route(xb_ref[...], rwt_ref, rb_ref, tri_ref, carry_ref, code_ref, gate_ref, cnt_ref,
           (b == 0) & (s == 0))


def _const_spec(shape):
    return pl.BlockSpec(shape, lambda b, s: (0,) * len(shape))


def _route_out(n_steps_per_batch):
    t = MIX_TILE
    shapes = (jax.ShapeDtypeStruct((BATCH, SEQ, D_MODEL), F32),
              jax.ShapeDtypeStruct((TOP_K, N_TOK), jnp.int32),
              jax.ShapeDtypeStruct((TOP_K, N_TOK), F32),
              jax.ShapeDtypeStruct((N_EXPERTS, LANES), F32))
    specs = (pl.BlockSpec((1, t, D_MODEL), lambda b, s: (b, s, 0)),
             pl.BlockSpec((TOP_K, t), lambda b, s: (0, b * n_steps_per_batch + s)),
             pl.BlockSpec((TOP_K, t), lambda b, s: (0, b * n_steps_per_batch + s)),
             pl.BlockSpec((N_EXPERTS, LANES), lambda b, s: (0, 0)))
    return shapes, specs


def _conv_layer(x, w1, b1, dw, dwb, cg, cb, w2, b2, lg, lb, rwt, rb, tri):
    t = MIX_TILE
    steps = SEQ // t
    halo_per_tile = t // CONV_HALO
    out_shapes, out_specs = _route_out(steps)
    in_specs = [
        pl.BlockSpec((1, t, D_MODEL), lambda b, s: (b, s, 0)),
        pl.BlockSpec((1, CONV_HALO, D_MODEL),
                     lambda b, s: (b, jnp.maximum(s * halo_per_tile - 1, 0), 0)),
        _const_spec((D_MODEL, 2 * D_MODEL)), _const_spec((1, 2 * D_MODEL)),
        _const_spec((CONV_WIDTH, D_MODEL)), _const_spec((1, D_MODEL)),
        _const_spec((1, D_MODEL)), _const_spec((1, D_MODEL)),
        _const_spec((D_MODEL, D_MODEL)), _const_spec((1, D_MODEL)),
        _const_spec((1, D_MODEL)), _const_spec((1, D_MODEL)),
        _const_spec((N_EXPERTS, D_MODEL)), _const_spec((N_EXPERTS, 1)),
        _const_spec((t, t)),
    ]
    return pl.pallas_call(
        _conv_kernel,
        out_shape=out_shapes,
        grid=(BATCH, steps),
        in_specs=in_specs,
        out_specs=out_specs,
        scratch_shapes=[
            pltpu.VMEM((t + CONV_HALO, D_MODEL), BF16),
            pltpu.VMEM((t + CONV_HALO, D_MODEL), F32),
            pltpu.VMEM((t, D_MODEL), F32),
            pltpu.VMEM((t, D_MODEL), BF16),
            pltpu.VMEM((t, D_MODEL), BF16),
            pltpu.VMEM((N_EXPERTS, LANES), F32),
        ],
        compiler_params=pltpu.CompilerParams(
            dimension_semantics=("arbitrary", "arbitrary"), vmem_limit_bytes=VMEM_LIMIT),
        name="conv_mixer",
    )(x, x, w1, b1, dw, dwb, cg, cb, w2, b2, lg, lb, rwt, rb, tri)


def _rope_block(v, cos, sin_signed, lower_half):
    partner = jnp.where(lower_half, pltpu.roll(v, LANES - HEAD_DIM // 2, 1),
                        pltpu.roll(v, HEAD_DIM // 2, 1))
    return v * cos + partner * sin_signed


def _lower_half_mask(rows):
    lane = lax.broadcasted_iota(jnp.int32, (rows, LANES), 1)
    return (lane % HEAD_DIM) < (HEAD_DIM // 2)


def _kv_kernel(x_ref, pos_ref, invf_ref, w_ref, b_ref, k_ref, v_ref, cos_ref, sin_ref):
    t = KV_TILE
    kv = jnp.dot(x_ref[...].astype(BF16), w_ref[...], preferred_element_type=F32) + b_ref[...]
    ang = pos_ref[...].astype(F32) * invf_ref[...]
    lower = _lower_half_mask(t)
    cos = jnp.cos(ang)
    sin = jnp.sin(ang)
    sin_signed = jnp.where(lower, -sin, sin)
    cos_ref[...] = cos
    sin_ref[...] = sin_signed
    for c in range(KV_DIM // LANES):
        cols = slice(c * LANES, (c + 1) * LANES)
        k_ref[:, cols] = _rope_block(kv[:, cols], cos, sin_signed, lower).astype(BF16)
    v_ref[...] = kv[:, KV_DIM:].astype(BF16)


def _shared_kv(x2d, pos_col, invf, kv_w, kv_b):
    t = KV_TILE
    return pl.pallas_call(
        _kv_kernel,
        out_shape=(jax.ShapeDtypeStruct((N_TOK, KV_DIM), BF16),
                   jax.ShapeDtypeStruct((N_TOK, KV_DIM), BF16),
                   jax.ShapeDtypeStruct((N_TOK, LANES), F32),
                   jax.ShapeDtypeStruct((N_TOK, LANES), F32)),
        grid=(N_TOK // t,),
        in_specs=[pl.BlockSpec((t, D_MODEL), lambda i: (i, 0)),
                  pl.BlockSpec((t, 1), lambda i: (i, 0)),
                  pl.BlockSpec((1, LANES), lambda i: (0, 0)),
                  pl.BlockSpec((D_MODEL, 2 * KV_DIM), lambda i: (0, 0)),
                  pl.BlockSpec((1, 2 * KV_DIM), lambda i: (0, 0))],
        out_specs=(pl.BlockSpec((t, KV_DIM), lambda i: (i, 0)),
                   pl.BlockSpec((t, KV_DIM), lambda i: (i, 0)),
                   pl.BlockSpec((t, LANES), lambda i: (i, 0)),
                   pl.BlockSpec((t, LANES), lambda i: (i, 0))),
        compiler_params=pltpu.CompilerParams(
            dimension_semantics=("arbitrary",), vmem_limit_bytes=VMEM_LIMIT),
        name="shared_kv",
    )(x2d, pos_col, invf, kv_w, kv_b)


def _attn_kernel(x_ref, kp_ref, k_ref, vp_ref, v_ref, cos_ref, sin_ref, wq_ref, bq_ref, sink_ref,
                 wo_ref, bo_ref, lg_ref, lb_ref, rwt_ref, rb_ref, tri_ref,
                 x1_ref, code_ref, gate_ref, cnt_ref,
                 q_ref, kf_ref, vf_ref, o_ref, mix_ref, xb_ref, carry_ref):
    b = pl.program_id(0)
    s = pl.program_id(1)
    t = MIX_TILE

    q = jnp.dot(x_ref[0].astype(BF16), wq_ref[...], preferred_element_type=F32) + bq_ref[...]
    lower = _lower_half_mask(t)
    cos = cos_ref[...]
    sin_signed = sin_ref[...]
    scale = 1.0 / math.sqrt(HEAD_DIM)
    for c in range(D_MODEL // LANES):
        cols = slice(c * LANES, (c + 1) * LANES)
        q_ref[:, cols] = (_rope_block(q[:, cols], cos, sin_signed, lower) * scale).astype(BF16)

    kf_ref[0:WINDOW, :] = kp_ref[...]
    kf_ref[WINDOW:, :] = k_ref[...]
    vf_ref[0:WINDOW, :] = vp_ref[...]
    vf_ref[WINDOW:, :] = v_ref[...]

    qi = lax.broadcasted_iota(jnp.int32, (WINDOW, 2 * WINDOW), 0)
    kj = lax.broadcasted_iota(jnp.int32, (WINDOW, 2 * WINDOW), 1)
    dist = qi + WINDOW - kj
    in_window = (dist >= 0) & (dist < WINDOW)

    def block_body(n, carry):
        r0 = pl.multiple_of(n * WINDOW, WINDOW)
        first_key = jnp.where((s == 0) & (n == 0), WINDOW, 0)
        mask = in_window & (kj >= first_key)
        for kh in range(N_KV_HEADS):
            k2 = kf_ref[pl.ds(r0, 2 * WINDOW), kh * HEAD_DIM:(kh + 1) * HEAD_DIM]
            v2 = vf_ref[pl.ds(r0, 2 * WINDOW), kh * HEAD_DIM:(kh + 1) * HEAD_DIM]
            outs = []
            for g in range(GQA_GROUP):
                h = kh * GQA_GROUP + g
                qh = q_ref[pl.ds(r0, WINDOW), h * HEAD_DIM:(h + 1) * HEAD_DIM]
                sc = lax.dot_general(qh, k2, (((1,), (1,)), ((), ())),
                                     preferred_element_type=F32)
                sc = jnp.where(mask, sc, NEG_INF)
                sink = sink_ref[h]
                m = jnp.maximum(jnp.max(sc, axis=-1, keepdims=True), sink)
                p = jnp.exp(sc - m)
                denom = jnp.sum(p, axis=-1, keepdims=True) + jnp.exp(sink - m)
                pv = jnp.dot(p.astype(BF16), v2, preferred_element_type=F32)
                outs.append(pv / denom)
            o_ref[pl.ds(r0, WINDOW), kh * GQA_GROUP * HEAD_DIM:(kh + 1) * GQA_GROUP * HEAD_DIM] = (
                jnp.concatenate(outs, axis=1).astype(BF16))
        return carry
    lax.fori_loop(0, t // WINDOW, block_body, 0)

    mix_ref[...] = jnp.dot(o_ref[...], wo_ref[...], preferred_element_type=F32) + bo_ref[...]
    _deepnorm_rows(x_ref, mix_ref, lg_ref, lb_ref, x1_ref, xb_ref)
    _route(xb_ref[...], rwt_ref, rb_ref, tri_ref, carry_ref, code_ref, gate_ref, cnt_ref,
           (b == 0) & (s == 0))


def _attn_layer(x, k, v, cos, sin, wq, bq, sinks, wo, bo, lg, lb, rwt, rb, tri):
    t = MIX_TILE
    steps = SEQ // t
    win_per_tile = t // WINDOW
    out_shapes, out_specs = _route_out(steps)

    def tile_map(b, s):
        return (b * steps + s, 0)

    def prev_window_map(b, s):
        return (b * (SEQ // WINDOW) + jnp.maximum(s * win_per_tile - 1, 0), 0)

    in_specs = [
        pl.BlockSpec((1, t, D_MODEL), lambda b, s: (b, s, 0)),
        pl.BlockSpec((WINDOW, KV_DIM), prev_window_map),
        pl.BlockSpec((t, KV_DIM), tile_map),
        pl.BlockSpec((WINDOW, KV_DIM), prev_window_map),
        pl.BlockSpec((t, KV_DIM), tile_map),
        pl.BlockSpec((t, LANES), tile_map),
        pl.BlockSpec((t, LANES), tile_map),
        _const_spec((D_MODEL, D_MODEL)), _const_spec((1, D_MODEL)),
        pl.BlockSpec(memory_space=pltpu.SMEM),
        _const_spec((D_MODEL, D_MODEL)), _const_spec((1, D_MODEL)),
        _const_spec((1, D_MODEL)), _const_spec((1, D_MODEL)),
        _const_spec((N_EXPERTS, D_MODEL)), _const_spec((N_EXPERTS, 1)),
        _const_spec((t, t)),
    ]
    return pl.pallas_call(
        _attn_kernel,
        out_shape=out_shapes,
        grid=(BATCH, steps),
        in_specs=in_specs,
        out_specs=out_specs,
        scratch_shapes=[
            pltpu.VMEM((t, D_MODEL), BF16),
            pltpu.VMEM((t + WINDOW, KV_DIM), BF16),
            pltpu.VMEM((t + WINDOW, KV_DIM), BF16),
            pltpu.VMEM((t, D_MODEL), BF16),
            pltpu.VMEM((t, D_MODEL), F32),
            pltpu.VMEM((t, D_MODEL), BF16),
            pltpu.VMEM((N_EXPERTS, LANES), F32),
        ],
        compiler_params=pltpu.CompilerParams(
            dimension_semantics=("arbitrary", "arbitrary"), vmem_limit_bytes=VMEM_LIMIT),
        name="attn_mixer",
    )(x, k, k, v, v, cos, sin, wq, bq, sinks, wo, bo, lg, lb, rwt, rb, tri)


def _slot_of(code_ref, pstart_ref, idx):
    code = code_ref[idx]
    return pstart_ref[code >> 16] + (code & 0xFFFF)


def _zero_fill_slots(fill_ref, pend_ref, xs_ref, zero_ref, zsem):
    zero_ref[...] = jnp.zeros_like(zero_ref)

    def pad_row(r):
        return pltpu.make_async_copy(zero_ref.at[pl.ds(0, 1)], xs_ref.at[pl.ds(r, 1)], zsem)

    def tail_block(j):
        rows = pl.ds(pl.multiple_of(j * EXPERT_BLOCK, EXPERT_BLOCK), EXPERT_BLOCK)
        return pltpu.make_async_copy(zero_ref, xs_ref.at[rows], zsem)

    first_tail = pend_ref[N_EXPERTS - 1] // EXPERT_BLOCK
    for e in range(N_EXPERTS):
        lax.fori_loop(fill_ref[e], pend_ref[e], lambda r, c: (pad_row(r).start(), c)[1], 0)
    lax.fori_loop(first_tail, N_BLOCKS, lambda j, c: (tail_block(j).start(), c)[1], 0)
    for e in range(N_EXPERTS):
        lax.fori_loop(fill_ref[e], pend_ref[e], lambda r, c: (pad_row(r).wait(), c)[1], 0)
    lax.fori_loop(first_tail, N_BLOCKS, lambda j, c: (tail_block(j).wait(), c)[1], 0)


def _dispatch_kernel(code_ref, pstart_ref, fill_ref, pend_ref, x_ref, xs_ref, zero_ref, sem, zsem):
    base = pl.program_id(0) * MOVE_TILE

    @pl.when(pl.program_id(0) == 0)
    def _():
        _zero_fill_slots(fill_ref, pend_ref, xs_ref, zero_ref, zsem)

    def row_copy(t, k):
        slot = _slot_of(code_ref, pstart_ref, k * N_TOK + base + t)
        return pltpu.make_async_copy(x_ref.at[pl.ds(t, 1)], xs_ref.at[pl.ds(slot, 1)], sem)

    def issue(t, carry):
        for k in range(TOP_K):
            row_copy(t, k).start()
        return carry
    lax.fori_loop(0, MOVE_TILE, issue, 0, unroll=8)

    for _ in range(TOP_K):
        pltpu.make_async_copy(x_ref, xs_ref.at[pl.ds(0, MOVE_TILE)], sem).wait()


def _dispatch(codes, pstart, fill, pend, x2d):
    t = MOVE_TILE
    grid_spec = pltpu.PrefetchScalarGridSpec(
        num_scalar_prefetch=4,
        grid=(N_TOK // t,),
        in_specs=[pl.BlockSpec((t, D_MODEL), lambda i, *_: (i, 0))],
        out_specs=pl.BlockSpec(memory_space=pl.ANY),
        scratch_shapes=[pltpu.VMEM((EXPERT_BLOCK, D_MODEL), F32),
                        pltpu.SemaphoreType.DMA, pltpu.SemaphoreType.DMA],
    )
    return pl.pallas_call(
        _dispatch_kernel,
        out_shape=jax.ShapeDtypeStruct((N_SLOTS, D_MODEL), F32),
        grid_spec=grid_spec,
        compiler_params=pltpu.CompilerParams(
            dimension_semantics=("arbitrary",), vmem_limit_bytes=VMEM_LIMIT),
        name="moe_dispatch",
    )(codes, pstart, fill, pend, x2d)


def _expert_kernel(be_ref, bv_ref, xs_ref, wg_ref, wu_ref, wd_ref, ys_ref, wg_s, wu_s, wd_s):
    i = pl.program_id(0)
    valid = bv_ref[i]
    new_expert = (i == 0) | (be_ref[i] != be_ref[jnp.maximum(i - 1, 0)])

    @pl.when(new_expert & (valid > 0))
    def _():
        wg_s[...] = wg_ref[...].astype(BF16)
        wu_s[...] = wu_ref[...].astype(BF16)
        wd_s[...] = wd_ref[...].astype(BF16)

    @pl.when(valid > 0)
    def _():
        row = lax.broadcasted_iota(jnp.int32, (EXPERT_BLOCK, 1), 0)
        x = jnp.where(row < valid, xs_ref[...], 0.0).astype(BF16)
        hg = jnp.dot(x, wg_s[...], preferred_element_type=F32)
        hu = jnp.dot(x, wu_s[...], preferred_element_type=F32)
        h = (hg * jax.nn.sigmoid(hg) * hu).astype(BF16)
        ys_ref[...] = jnp.dot(h, wd_s[...], preferred_element_type=F32)

    @pl.when(valid <= 0)
    def _():
        ys_ref[...] = jnp.zeros_like(ys_ref)


def _experts(layer, blk_expert, blk_valid, xs, w_gate, w_up, w_down):
    grid_spec = pltpu.PrefetchScalarGridSpec(
        num_scalar_prefetch=2,
        grid=(N_BLOCKS,),
        in_specs=[
            pl.BlockSpec((EXPERT_BLOCK, D_MODEL), lambda i, be, bv: (i, 0)),
            pl.BlockSpec((None, None, D_MODEL, D_EXPERT), lambda i, be, bv: (layer, be[i], 0, 0)),
            pl.BlockSpec((None, None, D_MODEL, D_EXPERT), lambda i, be, bv: (layer, be[i], 0, 0)),
            pl.BlockSpec((None, None, D_EXPERT, D_MODEL), lambda i, be, bv: (layer, be[i], 0, 0)),
        ],
        out_specs=pl.BlockSpec((EXPERT_BLOCK, D_MODEL), lambda i, be, bv: (i, 0)),
        scratch_shapes=[pltpu.VMEM((D_MODEL, D_EXPERT), BF16),
                        pltpu.VMEM((D_MODEL, D_EXPERT), BF16),
                        pltpu.VMEM((D_EXPERT, D_MODEL), BF16)],
    )
    return pl.pallas_call(
        _expert_kernel,
        out_shape=jax.ShapeDtypeStruct((N_SLOTS, D_MODEL), F32),
        grid_spec=grid_spec,
        compiler_params=pltpu.CompilerParams(
            dimension_semantics=("arbitrary",), vmem_limit_bytes=VMEM_LIMIT),
        name="moe_experts",
    )(blk_expert, blk_valid, xs, w_gate, w_up, w_down)


def _combine_kernel(code_ref, pstart_ref, x_ref, gate_ref, ys_ref, lg_ref, lb_ref, out_ref,
                    buf_ref, sem):
    base = pl.program_id(0) * MOVE_TILE

    def row_copy(t, k):
        slot = _slot_of(code_ref, pstart_ref, k * N_TOK + base + t)
        return pltpu.make_async_copy(ys_ref.at[pl.ds(slot, 1)], buf_ref.at[k, pl.ds(t, 1)], sem)

    def issue(t, carry):
        for k in range(TOP_K):
            row_copy(t, k).start()
        return carry
    lax.fori_loop(0, MOVE_TILE, issue, 0, unroll=8)

    for k in range(TOP_K):
        pltpu.make_async_copy(ys_ref.at[pl.ds(0, MOVE_TILE)], buf_ref.at[k], sem).wait()

    def body(i, carry):
        rows = pl.ds(pl.multiple_of(i * ROW_CHUNK, ROW_CHUNK), ROW_CHUNK)
        g = gate_ref[rows, :]
        y = g[:, 0:1] * buf_ref[0, rows, :] + g[:, 1:2] * buf_ref[1, rows, :]
        z = DEEPNORM_ALPHA * x_ref[rows, :] + y
        out_ref[rows, :] = _layer_norm(z, lg_ref[...], lb_ref[...])
        return carry
    lax.fori_loop(0, MOVE_TILE // ROW_CHUNK, body, 0)


def _combine(codes, pstart, x2d, gate_col, ys, lg, lb):
    t = MOVE_TILE
    grid_spec = pltpu.PrefetchScalarGridSpec(
        num_scalar_prefetch=2,
        grid=(N_TOK // t,),
        in_specs=[pl.BlockSpec((t, D_MODEL), lambda i, c, p: (i, 0)),
                  pl.BlockSpec((t, TOP_K), lambda i, c, p: (i, 0)),
                  pl.BlockSpec(memory_space=pl.ANY),
                  pl.BlockSpec((1, D_MODEL), lambda i, c, p: (0, 0)),
                  pl.BlockSpec((1, D_MODEL), lambda i, c, p: (0, 0))],
        out_specs=pl.BlockSpec((t, D_MODEL), lambda i, c, p: (i, 0)),
        scratch_shapes=[pltpu.VMEM((TOP_K, t, D_MODEL), F32), pltpu.SemaphoreType.DMA],
    )
    return pl.pallas_call(
        _combine_kernel,
        out_shape=jax.ShapeDtypeStruct((N_TOK, D_MODEL), F32),
        grid_spec=grid_spec,
        compiler_params=pltpu.CompilerParams(
            dimension_semantics=("arbitrary",), vmem_limit_bytes=VMEM_LIMIT),
        name="moe_combine",
    )(codes, pstart, x2d, gate_col, ys, lg, lb)


def _moe(layer, x1, codes, gates, counts, w_gate, w_up, w_down, lg, lb):
    cnt = counts[:, 0].astype(jnp.int32)
    padded = ((cnt + EXPERT_BLOCK - 1) // EXPERT_BLOCK) * EXPERT_BLOCK
    pend = jnp.cumsum(padded)
    pstart = (pend - padded).astype(jnp.int32)
    blk_start = jnp.arange(N_BLOCKS, dtype=jnp.int32) * EXPERT_BLOCK
    blk_expert = jnp.minimum(jnp.searchsorted(pend, blk_start, side='right'),
                             N_EXPERTS - 1).astype(jnp.int32)
    blk_valid = jnp.clip(pstart[blk_expert] + cnt[blk_expert] - blk_start,
                         0, EXPERT_BLOCK).astype(jnp.int32)
    codes_flat = codes.reshape(TOP_K * N_TOK)
    x2d = x1.reshape(N_TOK, D_MODEL)
    xs = _dispatch(codes_flat, pstart, pstart + cnt, pend.astype(jnp.int32), x2d)
    ys = _experts(layer, blk_expert, blk_valid, xs, w_gate, w_up, w_down)
    out = _combine(codes_flat, pstart, x2d, gates.T, ys, lg, lb)
    return out.reshape(BATCH, SEQ, D_MODEL)


def kernel(x, positions, conv_w1, conv_b1, conv_dw, conv_dwb, conv_ln_g, conv_ln_b, conv_w2,
           conv_b2, kv_w, kv_b, attn_wq, attn_bq, attn_sinks, attn_wo, attn_bo, router_w,
           router_b, moe_w_gate, moe_w_up, moe_w_down, ln_g, ln_b):
    row = lambda v: v.reshape(1, -1)
    rwt = router_w.T.astype(BF16)
    rb = router_b.astype(F32).reshape(N_EXPERTS, 1)
    idx = jnp.arange(MIX_TILE, dtype=jnp.int32)
    tri = (idx[:, None] < idx[None, :]).astype(BF16)
    inv_freq = ROPE_THETA ** (-jnp.arange(0, HEAD_DIM, 2, dtype=F32) / HEAD_DIM)
    invf = jnp.tile(inv_freq, LANES // (HEAD_DIM // 2)).reshape(1, LANES)
    pos_col = positions.reshape(N_TOK, 1)

    k = v = cos = sin = None
    for layer in range(DEPTH):
        lg0, lb0 = row(ln_g[layer, 0]), row(ln_b[layer, 0])
        if layer < N_CONV_LAYERS:
            i = layer
            x1, codes, gates, counts = _conv_layer(
                x, conv_w1[i].astype(BF16), row(conv_b1[i]), conv_dw[i], row(conv_dwb[i]),
                row(conv_ln_g[i]), row(conv_ln_b[i]), conv_w2[i].astype(BF16), row(conv_b2[i]),
                lg0, lb0, rwt, rb, tri)
        else:
            if layer == N_CONV_LAYERS:
                k, v, cos, sin = _shared_kv(x.reshape(N_TOK, D_MODEL), pos_col, invf,
                                            kv_w.astype(BF16), row(kv_b))
            j = layer - N_CONV_LAYERS
            x1, codes, gates, counts = _attn_layer(
                x, k, v, cos, sin, attn_wq[j].astype(BF16), row(attn_bq[j]),
                attn_sinks[j].astype(F32), attn_wo[j].astype(BF16), row(attn_bo[j]),
                lg0, lb0, rwt, rb, tri)
        x = _moe(layer, x1, codes, gates, counts, moe_w_gate, moe_w_up, moe_w_down,
                 row(ln_g[layer, 1]), row(ln_b[layer, 1]))
    return x
```

```python
import math

import jax
import jax.numpy as jnp
from jax import lax
from jax.experimental import pallas as pl
from jax.experimental.pallas import tpu as pltpu

D_MODEL = 1024
BATCH = 8
SEQ = 4096
DEPTH = 4
N_TOK = BATCH * SEQ
N_CONV_LAYERS = DEPTH // 2
CONV_WIDTH = 31
HEAD_DIM = 64
N_Q_HEADS = D_MODEL // HEAD_DIM
N_KV_HEADS = 4
GQA_GROUP = N_Q_HEADS // N_KV_HEADS
KV_DIM = N_KV_HEADS * HEAD_DIM
WINDOW = 128
ROPE_THETA = 10000.0
N_EXPERTS = 16
N_GROUPS = 4
EXPERTS_PER_GROUP = N_EXPERTS // N_GROUPS
TOP_K = 2
D_EXPERT = D_MODEL // 2
LN_EPS = 1e-5
NEG_INF = -1e30
DEEPNORM_ALPHA = (2.0 * DEPTH) ** 0.25

LANES = 128
SUBLANES = 8
MXU_DIM = 256
N_LANE_BLOCKS = D_MODEL // LANES
D_PACK = D_MODEL // 2

MIX_TILE = 512
CONV_HALO = 32
ROW_CHUNK = 64
KV_TILE = 1024

SORT_TILE = 256
RUN_ALIGN = SUBLANES
SORT_SLOTS = TOP_K * SORT_TILE + LANES
N_SORT_TILES = N_TOK // SORT_TILE
SORT_PER_MIX = MIX_TILE // SORT_TILE
EXPERT_BLOCK = 256
_MAX_RUN_ROWS = N_TOK * TOP_K + N_SORT_TILES * N_EXPERTS * (RUN_ALIGN - 1)
N_BLOCKS = (_MAX_RUN_ROWS + N_EXPERTS * (EXPERT_BLOCK - 1)) // EXPERT_BLOCK + 1
N_SLOTS = N_BLOCKS * EXPERT_BLOCK

VMEM_LIMIT = 56 * 1024 * 1024

F32 = jnp.float32
BF16 = jnp.bfloat16
U32 = jnp.uint32
I32 = jnp.int32

assert SORT_SLOTS >= TOP_K * SORT_TILE + N_EXPERTS * (RUN_ALIGN - 1) and SORT_SLOTS % RUN_ALIGN == 0


def _layer_norm(z, g, b):
    mu = jnp.mean(z, axis=-1, keepdims=True)
    zc = z - mu
    var = jnp.mean(zc * zc, axis=-1, keepdims=True)
    return zc * lax.rsqrt(var + LN_EPS) * g + b


def _pack_rows(v):
    hi = lax.bitcast_convert_type(v[:, :D_PACK], U32) & jnp.uint32(0xFFFF0000)
    lo = lax.bitcast_convert_type(v[:, D_PACK:], U32) >> 16
    return hi | lo


def _unpack_rows(w):
    hi = lax.bitcast_convert_type(w & jnp.uint32(0xFFFF0000), F32)
    lo = lax.bitcast_convert_type(w << 16, F32)
    return jnp.concatenate([hi, lo], axis=1).astype(BF16)


def _top2_sum(a, b, c, d):
    hi1, lo1 = jnp.maximum(a, b), jnp.minimum(a, b)
    hi2, lo2 = jnp.maximum(c, d), jnp.minimum(c, d)
    top1 = jnp.maximum(hi1, hi2)
    top2 = jnp.maximum(jnp.minimum(hi1, hi2), jnp.maximum(lo1, lo2))
    return top1 + top2


def _argmax4(vals):
    best, idx = vals[0], jnp.zeros(vals[0].shape, I32)
    for j in range(1, 4):
        better = vals[j] > best
        idx = jnp.where(better, j, idx)
        best = jnp.where(better, vals[j], best)
    return idx


def _pick4(idx, vals):
    out = vals[3]
    for j in (2, 1, 0):
        out = jnp.where(idx == j, vals[j], out)
    return out


def _route_sort(half, xb, rwt_ref, rb_ref, tri_ref, ltri_ref, carry_ref,
                xsort_ref, lpos_ref, gate_ref, tstart_ref, tcnt_ref):
    t = SORT_TILE
    cols = slice(half * t, (half + 1) * t)
    logits = lax.dot_general(rwt_ref[...], xb, (((1,), (1,)), ((), ())),
                             preferred_element_type=F32)
    aff = jax.nn.sigmoid(logits)
    sel = aff + rb_ref[...]
    sel_rows = [sel[e:e + 1, :] for e in range(N_EXPERTS)]
    aff_rows = [aff[e:e + 1, :] for e in range(N_EXPERTS)]
    gscore = [_top2_sum(*sel_rows[4 * g:4 * g + 4]) for g in range(N_GROUPS)]
    grp = _argmax4(gscore)
    sel_in = [_pick4(grp, [sel_rows[4 * g + j] for g in range(N_GROUPS)]) for j in range(4)]
    aff_in = [_pick4(grp, [aff_rows[4 * g + j] for g in range(N_GROUPS)]) for j in range(4)]
    i0 = _argmax4(sel_in)
    i1 = _argmax4([jnp.where(i0 == j, -jnp.inf, sel_in[j]) for j in range(4)])
    a0 = _pick4(i0, aff_in)
    a1 = _pick4(i1, aff_in)
    gsum = a0 + a1
    gate_ref[0:1, cols] = a0 / gsum
    gate_ref[1:2, cols] = a1 / gsum
    e0 = grp * EXPERTS_PER_GROUP + i0
    e1 = grp * EXPERTS_PER_GROUP + i1

    eiota = lax.broadcasted_iota(I32, (N_EXPERTS, t), 0)
    hit0 = eiota == e0
    hit1 = eiota == e1
    onehot = jnp.where(hit0 | hit1, 1.0, 0.0)
    before = jnp.dot(onehot.astype(BF16), tri_ref[...], preferred_element_type=F32)
    cnt = jnp.sum(onehot, axis=1, keepdims=True)
    run = jnp.floor((cnt + (RUN_ALIGN - 1)) * (1.0 / RUN_ALIGN)) * RUN_ALIGN
    run_b = jnp.broadcast_to(run, (N_EXPERTS, LANES))
    loff = jnp.dot(ltri_ref[...], run_b.astype(BF16), preferred_element_type=F32)
    pos = before + loff[:, 0:1]
    lp0 = jnp.sum(jnp.where(hit0, pos, 0.0), axis=0, keepdims=True).astype(I32)
    lp1 = jnp.sum(jnp.where(hit1, pos, 0.0), axis=0, keepdims=True).astype(I32)
    lpos_ref[0:1, cols] = lp0
    lpos_ref[1:2, cols] = lp1

    slot = lax.broadcasted_iota(I32, (SORT_SLOTS, t), 0)
    perm = jnp.where((slot == lp0) | (slot == lp1), 1.0, 0.0).astype(BF16)
    rows = jnp.dot(perm, xb, preferred_element_type=F32)
    xsort_ref[half * SORT_SLOTS:(half + 1) * SORT_SLOTS, :] = _pack_rows(rows)

    tstart_ref[half] = carry_ref[...]
    tcnt_ref[half] = run_b
    carry_ref[...] = carry_ref[...] + run_b


def _deepnorm_route(x_ref, mix_ref, lg_ref, lb_ref, x1_ref, xb_ref, first, route_refs, out_refs):
    carry_ref = route_refs[-1]

    @pl.when(first)
    def _():
        carry_ref[...] = jnp.zeros_like(carry_ref)

    def body(i, c):
        rows = pl.ds(pl.multiple_of(i * ROW_CHUNK, ROW_CHUNK), ROW_CHUNK)
        z = DEEPNORM_ALPHA * x_ref[0, rows, :] + mix_ref[rows, :]
        y = _layer_norm(z, lg_ref[...], lb_ref[...])
        x1_ref[0, rows, :] = y
        xb_ref[rows, :] = y.astype(BF16)
        return c
    lax.fori_loop(0, MIX_TILE // ROW_CHUNK, body, 0)

    for half in range(SORT_PER_MIX):
        xb = xb_ref[half * SORT_TILE:(half + 1) * SORT_TILE, :]
        _route_sort(half, xb, *route_refs, *out_refs)


def _conv_kernel(x_ref, xh_ref, w1_ref, b1_ref, dw_ref, dwb_ref, cg_ref, cb_ref, w2_ref, b2_ref,
                 lg_ref, lb_ref, rwt_ref, rb_ref, tri_ref, ltri_ref,
                 x1_ref, xsort_ref, lpos_ref, gate_ref, tstart_ref, tcnt_ref, tot_ref,
                 lhs_ref, h_ref, c_ref, mix_ref, act_ref, xb_ref, carry_ref):
    b = pl.program_id(0)
    s = pl.program_id(1)
    t = MIX_TILE

    lhs_ref[0:CONV_HALO, :] = xh_ref[0].astype(BF16)
    lhs_ref[CONV_HALO:, :] = x_ref[0].astype(BF16)
    lhs = lhs_ref[...]
    for cp in range(D_MODEL // MXU_DIM):
        ca = slice(cp * MXU_DIM, (cp + 1) * MXU_DIM)
        cg = slice(D_MODEL + cp * MXU_DIM, D_MODEL + (cp + 1) * MXU_DIM)
        a = jnp.dot(lhs, w1_ref[:, ca], preferred_element_type=F32) + b1_ref[:, ca]
        gt = jnp.dot(lhs, w1_ref[:, cg], preferred_element_type=F32) + b1_ref[:, cg]
        h = a * jax.nn.sigmoid(gt)
        for j in range(MXU_DIM // LANES):
            h_ref[cp * (MXU_DIM // LANES) + j] = h[:, j * LANES:(j + 1) * LANES]

    @pl.when(s == 0)
    def _():
        h_ref[:, 0:CONV_HALO, :] = jnp.zeros((N_LANE_BLOCKS, CONV_HALO, LANES), F32)

    off0 = CONV_HALO - (CONV_WIDTH - 1)

    def conv_body(c, carry):
        for i in range(t // ROW_CHUNK):
            acc = jnp.zeros((ROW_CHUNK, LANES), F32)
            for k in range(CONV_WIDTH):
                r0 = i * ROW_CHUNK + off0 + k
                acc = acc + dw_ref[c, k:k + 1, :] * h_ref[c, r0:r0 + ROW_CHUNK, :]
            c_ref[c, i * ROW_CHUNK:(i + 1) * ROW_CHUNK, :] = acc
        return carry
    lax.fori_loop(0, N_LANE_BLOCKS, conv_body, 0)

    def act_body(i, carry):
        rows = pl.ds(pl.multiple_of(i * ROW_CHUNK, ROW_CHUNK), ROW_CHUNK)
        z = jnp.concatenate([c_ref[c, rows, :] for c in range(N_LANE_BLOCKS)], axis=1)
        y = _layer_norm(z + dwb_ref[...], cg_ref[...], cb_ref[...])
        act_ref[rows, :] = (y * jax.nn.sigmoid(y)).astype(BF16)
        return carry
    lax.fori_loop(0, t // ROW_CHUNK, act_body, 0)

    mix_ref[...] = jnp.dot(act_ref[...], w2_ref[...], preferred_element_type=F32) + b2_ref[...]
    _deepnorm_route(x_ref, mix_ref, lg_ref, lb_ref, x1_ref, xb_ref, (b == 0) & (s == 0),
                    (rwt_ref, rb_ref, tri_ref, ltri_ref, carry_ref),
                    (xsort_ref, lpos_ref, gate_ref, tstart_ref, tcnt_ref))
    tot_ref[...] = carry_ref[...]


def _const_spec(shape):
    return pl.BlockSpec(shape, lambda b, s: (0,) * len(shape))


def _route_consts_specs():
    return [_const_spec((N_EXPERTS, D_MODEL)), _const_spec((N_EXPERTS, 1)),
            _const_spec((SORT_TILE, SORT_TILE)), _const_spec((N_EXPERTS, N_EXPERTS))]


def _mixer_outputs(steps):
    t = MIX_TILE
    step = lambda b, s: b * steps + s
    shapes = (jax.ShapeDtypeStruct((BATCH, SEQ, D_MODEL), F32),
              jax.ShapeDtypeStruct((N_SORT_TILES * SORT_SLOTS, D_PACK), U32),
              jax.ShapeDtypeStruct((TOP_K, N_TOK), I32),
              jax.ShapeDtypeStruct((TOP_K, N_TOK), F32),
              jax.ShapeDtypeStruct((N_SORT_TILES, N_EXPERTS, LANES), F32),
              jax.ShapeDtypeStruct((N_SORT_TILES, N_EXPERTS, LANES), F32),
              jax.ShapeDtypeStruct((N_EXPERTS, LANES), F32))
    specs = (pl.BlockSpec((1, t, D_MODEL), lambda b, s: (b, s, 0)),
             pl.BlockSpec((SORT_PER_MIX * SORT_SLOTS, D_PACK), lambda b, s: (step(b, s), 0)),
             pl.BlockSpec((TOP_K, t), lambda b, s: (0, step(b, s))),
             pl.BlockSpec((TOP_K, t), lambda b, s: (0, step(b, s))),
             pl.BlockSpec((SORT_PER_MIX, N_EXPERTS, LANES), lambda b, s: (step(b, s), 0, 0)),
             pl.BlockSpec((SORT_PER_MIX, N_EXPERTS, LANES), lambda b, s: (step(b, s), 0, 0)),
             pl.BlockSpec((N_EXPERTS, LANES), lambda b, s: (0, 0)))
    return shapes, specs


def _conv_layer(x, w1, b1, dw3, dwb, cg, cb, w2, b2, lg, lb, route_consts):
    t = MIX_TILE
    steps = SEQ // t
    halo_per_tile = t // CONV_HALO
    out_shapes, out_specs = _mixer_outputs(steps)
    in_specs = [
        pl.BlockSpec((1, t, D_MODEL), lambda b, s: (b, s, 0)),
        pl.BlockSpec((1, CONV_HALO, D_MODEL),
                     lambda b, s: (b, jnp.maximum(s * halo_per_tile - 1, 0), 0)),
        _const_spec((D_MODEL, 2 * D_MODEL)), _const_spec((1, 2 * D_MODEL)),
        _const_spec((N_LANE_BLOCKS, CONV_WIDTH, LANES)), _const_spec((1, D_MODEL)),
        _const_spec((1, D_MODEL)), _const_spec((1, D_MODEL)),
        _const_spec((D_MODEL, D_MODEL)), _const_spec((1, D_MODEL)),
        _const_spec((1, D_MODEL)), _const_spec((1, D_MODEL)),
    ] + _route_consts_specs()
    return pl.pallas_call(
        _conv_kernel,
        out_shape=out_shapes,
        grid=(BATCH, steps),
        in_specs=in_specs,
        out_specs=out_specs,
        scratch_shapes=[
            pltpu.VMEM((t + CONV_HALO, D_MODEL), BF16),
            pltpu.VMEM((N_LANE_BLOCKS, t + CONV_HALO, LANES), F32),
            pltpu.VMEM((N_LANE_BLOCKS, t, LANES), F32),
            pltpu.VMEM((t, D_MODEL), F32),
            pltpu.VMEM((t, D_MODEL), BF16),
            pltpu.VMEM((t, D_MODEL), BF16),
            pltpu.VMEM((N_EXPERTS, LANES), F32),
        ],
        compiler_params=pltpu.CompilerParams(
            dimension_semantics=("arbitrary", "arbitrary"), vmem_limit_bytes=VMEM_LIMIT),
        name="conv_mixer",
    )(x, x, w1, b1, dw3, dwb, cg, cb, w2, b2, lg, lb, *route_consts)


def _rope_block(v, cos, sin_signed, lower_half):
    partner = jnp.where(lower_half, pltpu.roll(v, LANES - HEAD_DIM // 2, 1),
                        pltpu.roll(v, HEAD_DIM // 2, 1))
    return v * cos + partner * sin_signed


def _lower_half_mask(rows):
    lane = lax.broadcasted_iota(I32, (rows, LANES), 1)
    return (lane % HEAD_DIM) < (HEAD_DIM // 2)


def _kv_kernel(x_ref, pos_ref, invf_ref, w_ref, b_ref, k_ref, v_ref, cos_ref, sin_ref):
    t = KV_TILE
    kv = jnp.dot(x_ref[...].astype(BF16), w_ref[...], preferred_element_type=F32) + b_ref[...]
    ang = pos_ref[...].astype(F32) * invf_ref[...]
    lower = _lower_half_mask(t)
    cos = jnp.cos(ang)
    sin = jnp.sin(ang)
    sin_signed = jnp.where(lower, -sin, sin)
    cos_ref[...] = cos
    sin_ref[...] = sin_signed
    for c in range(KV_DIM // LANES):
        cols = slice(c * LANES, (c + 1) * LANES)
        k_ref[:, cols] = _rope_block(kv[:, cols], cos, sin_signed, lower).astype(BF16)
    v_ref[...] = kv[:, KV_DIM:].astype(BF16)


def _shared_kv(x2d, pos_col, invf, kv_w, kv_b):
    t = KV_TILE
    return pl.pallas_call(
        _kv_kernel,
        out_shape=(jax.ShapeDtypeStruct((N_TOK, KV_DIM), BF16),
                   jax.ShapeDtypeStruct((N_TOK, KV_DIM), BF16),
                   jax.ShapeDtypeStruct((N_TOK, LANES), F32),
                   jax.ShapeDtypeStruct((N_TOK, LANES), F32)),
        grid=(N_TOK // t,),
        in_specs=[pl.BlockSpec((t, D_MODEL), lambda i: (i, 0)),
                  pl.BlockSpec((t, 1), lambda i: (i, 0)),
                  pl.BlockSpec((1, LANES), lambda i: (0, 0)),
                  pl.BlockSpec((D_MODEL, 2 * KV_DIM), lambda i: (0, 0)),
                  pl.BlockSpec((1, 2 * KV_DIM), lambda i: (0, 0))],
        out_specs=(pl.BlockSpec((t, KV_DIM), lambda i: (i, 0)),
                   pl.BlockSpec((t, KV_DIM), lambda i: (i, 0)),
                   pl.BlockSpec((t, LANES), lambda i: (i, 0)),
                   pl.BlockSpec((t, LANES), lambda i: (i, 0))),
        compiler_params=pltpu.CompilerParams(
            dimension_semantics=("arbitrary",), vmem_limit_bytes=VMEM_LIMIT),
        name="shared_kv",
    )(x2d, pos_col, invf, kv_w, kv_b)


def _attn_kernel(x_ref, kp_ref, k_ref, vp_ref, v_ref, cos_ref, sin_ref, wq_ref, bq_ref, sink_ref,
                 wo_ref, bo_ref, lg_ref, lb_ref, rwt_ref, rb_ref, tri_ref, ltri_ref,
                 x1_ref, xsort_ref, lpos_ref, gate_ref, tstart_ref, tcnt_ref, tot_ref,
                 q_ref, kf_ref, vf_ref, o_ref, mix_ref, xb_ref, carry_ref):
    b = pl.program_id(0)
    s = pl.program_id(1)
    t = MIX_TILE

    q = jnp.dot(x_ref[0].astype(BF16), wq_ref[...], preferred_element_type=F32) + bq_ref[...]
    lower = _lower_half_mask(t)
    cos = cos_ref[...]
    sin_signed = sin_ref[...]
    scale = 1.0 / math.sqrt(HEAD_DIM)
    for c in range(N_LANE_BLOCKS):
        cols = slice(c * LANES, (c + 1) * LANES)
        q_ref[:, cols] = (_rope_block(q[:, cols], cos, sin_signed, lower) * scale).astype(BF16)

    kf_ref[0:WINDOW, :] = kp_ref[...]
    kf_ref[WINDOW:, :] = k_ref[...]
    vf_ref[0:WINDOW, :] = vp_ref[...]
    vf_ref[WINDOW:, :] = v_ref[...]

    qi = lax.broadcasted_iota(I32, (WINDOW, 2 * WINDOW), 0)
    kj = lax.broadcasted_iota(I32, (WINDOW, 2 * WINDOW), 1)
    dist = qi + WINDOW - kj
    in_window = (dist >= 0) & (dist < WINDOW)

    def block_body(n, carry):
        r0 = pl.multiple_of(n * WINDOW, WINDOW)
        first_key = jnp.where((s == 0) & (n == 0), WINDOW, 0)
        mask = in_window & (kj >= first_key)
        for kh in range(N_KV_HEADS):
            k2 = kf_ref[pl.ds(r0, 2 * WINDOW), kh * HEAD_DIM:(kh + 1) * HEAD_DIM]
            v2 = vf_ref[pl.ds(r0, 2 * WINDOW), kh * HEAD_DIM:(kh + 1) * HEAD_DIM]
            outs = []
            for g in range(GQA_GROUP):
                h = kh * GQA_GROUP + g
                qh = q_ref[pl.ds(r0, WINDOW), h * HEAD_DIM:(h + 1) * HEAD_DIM]
                sc = lax.dot_general(qh, k2, (((1,), (1,)), ((), ())),
                                     preferred_element_type=F32)
                sc = jnp.where(mask, sc, NEG_INF)
                sink = sink_ref[h]
                m = jnp.maximum(jnp.max(sc, axis=-1, keepdims=True), sink)
                p = jnp.exp(sc - m)
                denom = jnp.sum(p, axis=-1, keepdims=True) + jnp.exp(sink - m)
                pv = jnp.dot(p.astype(BF16), v2, preferred_element_type=F32)
                outs.append(pv / denom)
            o_ref[pl.ds(r0, WINDOW), kh * GQA_GROUP * HEAD_DIM:(kh + 1) * GQA_GROUP * HEAD_DIM] = (
                jnp.concatenate(outs, axis=1).astype(BF16))
        return carry
    lax.fori_loop(0, t // WINDOW, block_body, 0)

    mix_ref[...] = jnp.dot(o_ref[...], wo_ref[...], preferred_element_type=F32) + bo_ref[...]
    _deepnorm_route(x_ref, mix_ref, lg_ref, lb_ref, x1_ref, xb_ref, (b == 0) & (s == 0),
                    (rwt_ref, rb_ref, tri_ref, ltri_ref, carry_ref),
                    (xsort_ref, lpos_ref, gate_ref, tstart_ref, tcnt_ref))
    tot_ref[...] = carry_ref[...]


def _attn_layer(x, k, v, cos, sin, wq, bq, sinks, wo, bo, lg, lb, route_consts):
    t = MIX_TILE
    steps = SEQ // t
    win_per_tile = t // WINDOW
    out_shapes, out_specs = _mixer_outputs(steps)

    def tile_map(b, s):
        return (b * steps + s, 0)

    def prev_window_map(b, s):
        return (b * (SEQ // WINDOW) + jnp.maximum(s * win_per_tile - 1, 0), 0)

    in_specs = [
        pl.BlockSpec((1, t, D_MODEL), lambda b, s: (b, s, 0)),
        pl.BlockSpec((WINDOW, KV_DIM), prev_window_map),
        pl.BlockSpec((t, KV_DIM), tile_map),
        pl.BlockSpec((WINDOW, KV_DIM), prev_window_map),
        pl.BlockSpec((t, KV_DIM), tile_map),
        pl.BlockSpec((t, LANES), tile_map),
        pl.BlockSpec((t, LANES), tile_map),
        _const_spec((D_MODEL, D_MODEL)), _const_spec((1, D_MODEL)),
        pl.BlockSpec(memory_space=pltpu.SMEM),
        _const_spec((D_MODEL, D_MODEL)), _const_spec((1, D_MODEL)),
        _const_spec((1, D_MODEL)), _const_spec((1, D_MODEL)),
    ] + _route_consts_specs()
    return pl.pallas_call(
        _attn_kernel,
        out_shape=out_shapes,
        grid=(BATCH, steps),
        in_specs=in_specs,
        out_specs=out_specs,
        scratch_shapes=[
            pltpu.VMEM((t, D_MODEL), BF16),
            pltpu.VMEM((t + WINDOW, KV_DIM), BF16),
            pltpu.VMEM((t + WINDOW, KV_DIM), BF16),
            pltpu.VMEM((t, D_MODEL), BF16),
            pltpu.VMEM((t, D_MODEL), F32),
            pltpu.VMEM((t, D_MODEL), BF16),
            pltpu.VMEM((N_EXPERTS, LANES), F32),
        ],
        compiler_params=pltpu.CompilerParams(
            dimension_semantics=("arbitrary", "arbitrary"), vmem_limit_bytes=VMEM_LIMIT),
        name="attn_mixer",
    )(x, k, k, v, v, cos, sin, wq, bq, sinks, wo, bo, lg, lb, *route_consts)


def _run_copies(to_slots, tstart_ref, tcnt_ref, pstart_ref, sorted_ref, slots_ref, sem):
    def copy(i, e, loff):
        n = pl.multiple_of(tcnt_ref[i * N_EXPERTS + e], RUN_ALIGN)
        local = pl.ds(pl.multiple_of(i * SORT_SLOTS + loff, RUN_ALIGN), n)
        glob = pl.ds(pl.multiple_of(pstart_ref[e] + tstart_ref[i * N_EXPERTS + e], RUN_ALIGN), n)
        src, dst = (sorted_ref.at[local], slots_ref.at[glob]) if to_slots else (
            slots_ref.at[glob], sorted_ref.at[local])
        return pltpu.make_async_copy(src, dst, sem), n

    def issue_tile(i):
        loff = jnp.int32(0)
        for e in range(N_EXPERTS):
            cp, n = copy(i, e, loff)

            @pl.when(n > 0)
            def _():
                cp.start()
            loff = loff + n
        return loff

    def wait_rows(n):
        n = pl.multiple_of(n, RUN_ALIGN)

        @pl.when(n > 0)
        def _():
            pltpu.make_async_copy(sorted_ref.at[pl.ds(0, n)], slots_ref.at[pl.ds(0, n)], sem).wait()

    def body(i, prev_rows):
        rows = issue_tile(i)
        wait_rows(prev_rows)
        return rows
    last_rows = lax.fori_loop(0, N_SORT_TILES, body, jnp.int32(0))
    wait_rows(last_rows)


def _dispatch_kernel(tstart_ref, tcnt_ref, pstart_ref, fill_ref, pend_ref, xsort_ref, xs_ref,
                     zero_ref, sem, zsem):
    zero_ref[...] = jnp.zeros_like(zero_ref)

    def pad_copy(e):
        n = pl.multiple_of(pend_ref[e] - fill_ref[e], RUN_ALIGN)
        dst = xs_ref.at[pl.ds(pl.multiple_of(fill_ref[e], RUN_ALIGN), n)]
        return pltpu.make_async_copy(zero_ref.at[pl.ds(0, n)], dst, zsem), n

    def tail_copy(j):
        rows = pl.ds(pl.multiple_of(j * EXPERT_BLOCK, EXPERT_BLOCK), EXPERT_BLOCK)
        return pltpu.make_async_copy(zero_ref, xs_ref.at[rows], zsem)

    first_tail = pend_ref[N_EXPERTS - 1] // EXPERT_BLOCK
    for e in range(N_EXPERTS):
        cp, n = pad_copy(e)
        pl.when(n > 0)(cp.start)
    lax.fori_loop(first_tail, N_BLOCKS, lambda j, c: (tail_copy(j).start(), c)[1], 0)

    _run_copies(True, tstart_ref, tcnt_ref, pstart_ref, xsort_ref, xs_ref, sem)

    for e in range(N_EXPERTS):
        cp, n = pad_copy(e)
        pl.when(n > 0)(cp.wait)
    lax.fori_loop(first_tail, N_BLOCKS, lambda j, c: (tail_copy(j).wait(), c)[1], 0)


def _dispatch(tstart, tcnt, pstart, fill, pend, xsort):
    grid_spec = pltpu.PrefetchScalarGridSpec(
        num_scalar_prefetch=5,
        grid=(1,),
        in_specs=[pl.BlockSpec(memory_space=pl.ANY)],
        out_specs=pl.BlockSpec(memory_space=pl.ANY),
        scratch_shapes=[pltpu.VMEM((EXPERT_BLOCK, D_PACK), U32),
                        pltpu.SemaphoreType.DMA, pltpu.SemaphoreType.DMA],
    )
    return pl.pallas_call(
        _dispatch_kernel,
        out_shape=jax.ShapeDtypeStruct((N_SLOTS, D_PACK), U32),
        grid_spec=grid_spec,
        compiler_params=pltpu.CompilerParams(
            dimension_semantics=("arbitrary",), vmem_limit_bytes=VMEM_LIMIT),
        name="moe_dispatch",
    )(tstart, tcnt, pstart, fill, pend, xsort)


def _undispatch_kernel(tstart_ref, tcnt_ref, pstart_ref, ys_ref, ysort_ref, zero_ref, sem, zsem):
    zero_ref[...] = jnp.zeros_like(zero_ref)

    def tail_copy(i):
        used = jnp.int32(0)
        for e in range(N_EXPERTS):
            used = used + tcnt_ref[i * N_EXPERTS + e]
        n = pl.multiple_of(SORT_SLOTS - used, RUN_ALIGN)
        dst = ysort_ref.at[pl.ds(pl.multiple_of(i * SORT_SLOTS + used, RUN_ALIGN), n)]
        return pltpu.make_async_copy(zero_ref.at[pl.ds(0, n)], dst, zsem)

    lax.fori_loop(0, N_SORT_TILES, lambda i, c: (tail_copy(i).start(), c)[1], 0)
    _run_copies(False, tstart_ref, tcnt_ref, pstart_ref, ysort_ref, ys_ref, sem)
    lax.fori_loop(0, N_SORT_TILES, lambda i, c: (tail_copy(i).wait(), c)[1], 0)


def _undispatch(tstart, tcnt, pstart, ys):
    grid_spec = pltpu.PrefetchScalarGridSpec(
        num_scalar_prefetch=3,
        grid=(1,),
        in_specs=[pl.BlockSpec(memory_space=pl.ANY)],
        out_specs=pl.BlockSpec(memory_space=pl.ANY),
        scratch_shapes=[pltpu.VMEM((SORT_SLOTS, D_PACK), U32),
                        pltpu.SemaphoreType.DMA, pltpu.SemaphoreType.DMA],
    )
    return pl.pallas_call(
        _undispatch_kernel,
        out_shape=jax.ShapeDtypeStruct((N_SORT_TILES * SORT_SLOTS, D_PACK), U32),
        grid_spec=grid_spec,
        compiler_params=pltpu.CompilerParams(
            dimension_semantics=("arbitrary",), vmem_limit_bytes=VMEM_LIMIT),
        name="moe_undispatch",
    )(tstart, tcnt, pstart, ys)


def _expert_kernel(be_ref, nb_ref, xs_ref, wg_ref, wu_ref, wd_ref, ys_ref, wg_s, wu_s, wd_s):
    i = pl.program_id(0)
    live = i < nb_ref[0]
    new_expert = (i == 0) | (be_ref[i] != be_ref[jnp.maximum(i - 1, 0)])

    @pl.when(new_expert & live)
    def _():
        wg_s[...] = wg_ref[...].astype(BF16)
        wu_s[...] = wu_ref[...].astype(BF16)
        wd_s[...] = wd_ref[...].astype(BF16)

    @pl.when(live)
    def _():
        x = _unpack_rows(xs_ref[...])
        hg = jnp.dot(x, wg_s[...], preferred_element_type=F32)
        hu = jnp.dot(x, wu_s[...], preferred_element_type=F32)
        h = (hg * jax.nn.sigmoid(hg) * hu).astype(BF16)
        y = jnp.dot(h, wd_s[...], preferred_element_type=F32)
        ys_ref[...] = _pack_rows(y.astype(BF16).astype(F32))

    @pl.when(jnp.logical_not(live))
    def _():
        ys_ref[...] = jnp.zeros_like(ys_ref)


def _experts(layer, blk_expert, n_live, xs, w_gate, w_up, w_down):
    def x_map(i, be, nb):
        return (jnp.minimum(i, jnp.maximum(nb[0] - 1, 0)), 0)

    w_map = lambda i, be, nb: (layer, be[i], 0, 0)
    grid_spec = pltpu.PrefetchScalarGridSpec(
        num_scalar_prefetch=2,
        grid=(N_BLOCKS,),
        in_specs=[
            pl.BlockSpec((EXPERT_BLOCK, D_PACK), x_map),
            pl.BlockSpec((None, None, D_MODEL, D_EXPERT), w_map),
            pl.BlockSpec((None, None, D_MODEL, D_EXPERT), w_map),
            pl.BlockSpec((None, None, D_EXPERT, D_MODEL), w_map),
        ],
        out_specs=pl.BlockSpec((EXPERT_BLOCK, D_PACK), lambda i, be, nb: (i, 0)),
        scratch_shapes=[pltpu.VMEM((D_MODEL, D_EXPERT), BF16),
                        pltpu.VMEM((D_MODEL, D_EXPERT), BF16),
                        pltpu.VMEM((D_EXPERT, D_MODEL), BF16)],
    )
    return pl.pallas_call(
        _expert_kernel,
        out_shape=jax.ShapeDtypeStruct((N_SLOTS, D_PACK), U32),
        grid_spec=grid_spec,
        compiler_params=pltpu.CompilerParams(
            dimension_semantics=("arbitrary",), vmem_limit_bytes=VMEM_LIMIT),
        name="moe_experts",
    )(blk_expert, n_live, xs, w_gate, w_up, w_down)


def _combine_kernel(x_ref, ysort_ref, lpos_ref, gate_ref, lg_ref, lb_ref, out_ref, y_ref):
    t = SORT_TILE
    ys = _unpack_rows(ysort_ref[...])
    slot = lax.broadcasted_iota(I32, (t, SORT_SLOTS), 1)
    lpos = lpos_ref[...]
    gate = gate_ref[...]
    y = None
    for k in range(TOP_K):
        unsort = jnp.where(slot == lpos[:, k:k + 1], 1.0, 0.0).astype(BF16)
        yk = gate[:, k:k + 1] * jnp.dot(unsort, ys, preferred_element_type=F32)
        y = yk if y is None else y + yk
    y_ref[...] = y

    def body(i, carry):
        rows = pl.ds(pl.multiple_of(i * ROW_CHUNK, ROW_CHUNK), ROW_CHUNK)
        z = DEEPNORM_ALPHA * x_ref[rows, :] + y_ref[rows, :]
        out_ref[rows, :] = _layer_norm(z, lg_ref[...], lb_ref[...])
        return carry
    lax.fori_loop(0, t // ROW_CHUNK, body, 0)


def _combine(x2d, ysort, lpos_col, gate_col, lg, lb):
    t = SORT_TILE
    return pl.pallas_call(
        _combine_kernel,
        out_shape=jax.ShapeDtypeStruct((N_TOK, D_MODEL), F32),
        grid=(N_SORT_TILES,),
        in_specs=[pl.BlockSpec((t, D_MODEL), lambda i: (i, 0)),
                  pl.BlockSpec((SORT_SLOTS, D_PACK), lambda i: (i, 0)),
                  pl.BlockSpec((t, TOP_K), lambda i: (i, 0)),
                  pl.BlockSpec((t, TOP_K), lambda i: (i, 0)),
                  pl.BlockSpec((1, D_MODEL), lambda i: (0, 0)),
                  pl.BlockSpec((1, D_MODEL), lambda i: (0, 0))],
        out_specs=pl.BlockSpec((t, D_MODEL), lambda i: (i, 0)),
        scratch_shapes=[pltpu.VMEM((t, D_MODEL), F32)],
        compiler_params=pltpu.CompilerParams(
            dimension_semantics=("arbitrary",), vmem_limit_bytes=VMEM_LIMIT),
        name="moe_combine",
    )(x2d, ysort, lpos_col, gate_col, lg, lb)


def _moe(layer, x1, xsort, lpos, gates, tstart, tcnt, tot, w_gate, w_up, w_down, lg, lb):
    tot = tot[:, 0].astype(I32)
    region = ((tot + EXPERT_BLOCK - 1) // EXPERT_BLOCK) * EXPERT_BLOCK
    pend = jnp.cumsum(region).astype(I32)
    pstart = pend - region
    n_live = (pend[N_EXPERTS - 1:] // EXPERT_BLOCK).astype(I32)
    blk_start = jnp.arange(N_BLOCKS, dtype=I32) * EXPERT_BLOCK
    blk_expert = jnp.minimum(jnp.sum(blk_start[:, None] >= pend[None, :], axis=1),
                             N_EXPERTS - 1).astype(I32)
    tstart_flat = tstart[:, :, 0].astype(I32).reshape(-1)
    tcnt_flat = tcnt[:, :, 0].astype(I32).reshape(-1)

    xs = _dispatch(tstart_flat, tcnt_flat, pstart, pstart + tot, pend, xsort)
    ys = _experts(layer, blk_expert, n_live, xs, w_gate, w_up, w_down)
    ysort = _undispatch(tstart_flat, tcnt_flat, pstart, ys)
    out = _combine(x1.reshape(N_TOK, D_MODEL), ysort, lpos.T, gates.T, lg, lb)
    return out.reshape(BATCH, SEQ, D_MODEL)


def kernel(x, positions, conv_w1, conv_b1, conv_dw, conv_dwb, conv_ln_g, conv_ln_b, conv_w2,
           conv_b2, kv_w, kv_b, attn_wq, attn_bq, attn_sinks, attn_wo, attn_bo, router_w,
           router_b, moe_w_gate, moe_w_up, moe_w_down, ln_g, ln_b):
    row = lambda v: v.reshape(1, -1)
    rwt = router_w.T.astype(BF16)
    rb = router_b.astype(F32).reshape(N_EXPERTS, 1)
    idx = jnp.arange(SORT_TILE, dtype=I32)
    tri = (idx[:, None] < idx[None, :]).astype(BF16)
    eidx = jnp.arange(N_EXPERTS, dtype=I32)
    ltri = (eidx[None, :] < eidx[:, None]).astype(BF16)
    route_consts = (rwt, rb, tri, ltri)
    inv_freq = ROPE_THETA ** (-jnp.arange(0, HEAD_DIM, 2, dtype=F32) / HEAD_DIM)
    invf = jnp.tile(inv_freq, LANES // (HEAD_DIM // 2)).reshape(1, LANES)
    pos_col = positions.reshape(N_TOK, 1)

    k = v = cos = sin = None
    for layer in range(DEPTH):
        lg0, lb0 = row(ln_g[layer, 0]), row(ln_b[layer, 0])
        if layer < N_CONV_LAYERS:
            i = layer
            dw3 = conv_dw[i].reshape(CONV_WIDTH, N_LANE_BLOCKS, LANES).transpose(1, 0, 2)
            outs = _conv_layer(
                x, conv_w1[i].astype(BF16), row(conv_b1[i]), dw3, row(conv_dwb[i]),
                row(conv_ln_g[i]), row(conv_ln_b[i]), conv_w2[i].astype(BF16), row(conv_b2[i]),
                lg0, lb0, route_consts)
        else:
            if layer == N_CONV_LAYERS:
                k, v, cos, sin = _shared_kv(x.reshape(N_TOK, D_MODEL), pos_col, invf,
                                            kv_w.astype(BF16), row(kv_b))
            j = layer - N_CONV_LAYERS
            outs = _attn_layer(
                x, k, v, cos, sin, attn_wq[j].astype(BF16), row(attn_bq[j]),
                attn_sinks[j].astype(F32), attn_wo[j].astype(BF16), row(attn_bo[j]),
                lg0, lb0, route_consts)
        x = _moe(layer, *outs, moe_w_gate, moe_w_up, moe_w_down,
                 row(ln_g[layer, 1]), row(ln_b[layer, 1]))
    return x
```

```python
import math

import jax
import jax.numpy as jnp
from jax import lax
from jax.experimental import pallas as pl
from jax.experimental.pallas import tpu as pltpu

D_MODEL = 1024
BATCH = 8
SEQ = 4096
DEPTH = 4
N_TOK = BATCH * SEQ
N_CONV_LAYERS = DEPTH // 2
CONV_WIDTH = 31
HEAD_DIM = 64
N_Q_HEADS = D_MODEL // HEAD_DIM
N_KV_HEADS = 4
GQA_GROUP = N_Q_HEADS // N_KV_HEADS
KV_DIM = N_KV_HEADS * HEAD_DIM
WINDOW = 128
ROPE_THETA = 10000.0
N_EXPERTS = 16
N_GROUPS = 4
EXPERTS_PER_GROUP = N_EXPERTS // N_GROUPS
TOP_K = 2
D_EXPERT = D_MODEL // 2
LN_EPS = 1e-5
NEG_INF = -1e30
DEEPNORM_ALPHA = (2.0 * DEPTH) ** 0.25

LANES = 128
SUBLANES = 8
MXU_DIM = 256
N_LANE_BLOCKS = D_MODEL // LANES
D_PACK = D_MODEL // 2

MIX_TILE = 512
CONV_HALO = 32
ROW_CHUNK = 64
KV_TILE = 1024

SORT_TILE = 256
RUN_ALIGN = SUBLANES
SORT_SLOTS = TOP_K * SORT_TILE + LANES
N_SORT_TILES = N_TOK // SORT_TILE
SORT_PER_MIX = MIX_TILE // SORT_TILE
DISPATCH_GROUP = 4
EXPERT_BLOCK = 256
_MAX_RUN_ROWS = N_TOK * TOP_K + N_SORT_TILES * N_EXPERTS * (RUN_ALIGN - 1)
N_BLOCKS = (_MAX_RUN_ROWS + N_EXPERTS * (EXPERT_BLOCK - 1)) // EXPERT_BLOCK + 1
N_SLOTS = N_BLOCKS * EXPERT_BLOCK

VMEM_LIMIT = 56 * 1024 * 1024

F32 = jnp.float32
BF16 = jnp.bfloat16
U32 = jnp.uint32
I32 = jnp.int32

assert SORT_SLOTS >= TOP_K * SORT_TILE + N_EXPERTS * (RUN_ALIGN - 1) and SORT_SLOTS % RUN_ALIGN == 0


def _layer_norm(z, g, b):
    mu = jnp.mean(z, axis=-1, keepdims=True)
    zc = z - mu
    var = jnp.mean(zc * zc, axis=-1, keepdims=True)
    return zc * lax.rsqrt(var + LN_EPS) * g + b


def _pack_rows(v):
    hi = lax.bitcast_convert_type(v[:, :D_PACK], U32) & jnp.uint32(0xFFFF0000)
    lo = lax.bitcast_convert_type(v[:, D_PACK:], U32) >> 16
    return hi | lo


def _unpack_rows(w):
    hi = lax.bitcast_convert_type(w & jnp.uint32(0xFFFF0000), F32)
    lo = lax.bitcast_convert_type(w << 16, F32)
    return jnp.concatenate([hi, lo], axis=1).astype(BF16)


def _top2_sum(a, b, c, d):
    hi1, lo1 = jnp.maximum(a, b), jnp.minimum(a, b)
    hi2, lo2 = jnp.maximum(c, d), jnp.minimum(c, d)
    top1 = jnp.maximum(hi1, hi2)
    top2 = jnp.maximum(jnp.minimum(hi1, hi2), jnp.maximum(lo1, lo2))
    return top1 + top2


def _argmax4(vals):
    best, idx = vals[0], jnp.zeros(vals[0].shape, I32)
    for j in range(1, 4):
        better = vals[j] > best
        idx = jnp.where(better, j, idx)
        best = jnp.where(better, vals[j], best)
    return idx


def _pick4(idx, vals):
    out = vals[3]
    for j in (2, 1, 0):
        out = jnp.where(idx == j, vals[j], out)
    return out


def _route_sort(half, xb, rwt_ref, rb_ref, tri_ref, ltri_ref, carry_ref,
                xsort_ref, lpos_ref, gate_ref, tstart_ref, tcnt_ref):
    t = SORT_TILE
    cols = slice(half * t, (half + 1) * t)
    logits = lax.dot_general(rwt_ref[...], xb, (((1,), (1,)), ((), ())),
                             preferred_element_type=F32)
    aff = jax.nn.sigmoid(logits)
    sel = aff + rb_ref[...]
    sel_rows = [sel[e:e + 1, :] for e in range(N_EXPERTS)]
    aff_rows = [aff[e:e + 1, :] for e in range(N_EXPERTS)]
    gscore = [_top2_sum(*sel_rows[4 * g:4 * g + 4]) for g in range(N_GROUPS)]
    grp = _argmax4(gscore)
    sel_in = [_pick4(grp, [sel_rows[4 * g + j] for g in range(N_GROUPS)]) for j in range(4)]
    aff_in = [_pick4(grp, [aff_rows[4 * g + j] for g in range(N_GROUPS)]) for j in range(4)]
    i0 = _argmax4(sel_in)
    i1 = _argmax4([jnp.where(i0 == j, -jnp.inf, sel_in[j]) for j in range(4)])
    a0 = _pick4(i0, aff_in)
    a1 = _pick4(i1, aff_in)
    gsum = a0 + a1
    gate_ref[0:1, cols] = a0 / gsum
    gate_ref[1:2, cols] = a1 / gsum
    e0 = grp * EXPERTS_PER_GROUP + i0
    e1 = grp * EXPERTS_PER_GROUP + i1

    eiota = lax.broadcasted_iota(I32, (N_EXPERTS, t), 0)
    hit0 = eiota == e0
    hit1 = eiota == e1
    onehot = jnp.where(hit0 | hit1, 1.0, 0.0)
    before = jnp.dot(onehot.astype(BF16), tri_ref[...], preferred_element_type=F32)
    cnt = jnp.sum(onehot, axis=1, keepdims=True)
    run = jnp.floor((cnt + (RUN_ALIGN - 1)) * (1.0 / RUN_ALIGN)) * RUN_ALIGN
    run_b = jnp.broadcast_to(run, (N_EXPERTS, LANES))
    loff = jnp.dot(ltri_ref[...], run_b.astype(BF16), preferred_element_type=F32)
    pos = before + loff[:, 0:1]
    lp0 = jnp.sum(jnp.where(hit0, pos, 0.0), axis=0, keepdims=True).astype(I32)
    lp1 = jnp.sum(jnp.where(hit1, pos, 0.0), axis=0, keepdims=True).astype(I32)
    lpos_ref[0:1, cols] = lp0
    lpos_ref[1:2, cols] = lp1

    slot = lax.broadcasted_iota(I32, (SORT_SLOTS, t), 0)
    perm = jnp.where((slot == lp0) | (slot == lp1), 1.0, 0.0).astype(BF16)
    rows = jnp.dot(perm, xb, preferred_element_type=F32)
    xsort_ref[half * SORT_SLOTS:(half + 1) * SORT_SLOTS, :] = _pack_rows(rows)

    tstart_ref[half] = carry_ref[...]
    tcnt_ref[half] = run_b
    carry_ref[...] = carry_ref[...] + run_b


def _deepnorm_route(x_ref, mix_ref, lg_ref, lb_ref, x1_ref, xb_ref, first, route_refs, out_refs):
    carry_ref = route_refs[-1]

    @pl.when(first)
    def _():
        carry_ref[...] = jnp.zeros_like(carry_ref)

    def body(i, c):
        rows = pl.ds(pl.multiple_of(i * ROW_CHUNK, ROW_CHUNK), ROW_CHUNK)
        z = DEEPNORM_ALPHA * x_ref[0, rows, :] + mix_ref[rows, :]
        y = _layer_norm(z, lg_ref[...], lb_ref[...])
        x1_ref[0, rows, :] = y
        xb_ref[rows, :] = y.astype(BF16)
        return c
    lax.fori_loop(0, MIX_TILE // ROW_CHUNK, body, 0)

    for half in range(SORT_PER_MIX):
        xb = xb_ref[half * SORT_TILE:(half + 1) * SORT_TILE, :]
        _route_sort(half, xb, *route_refs, *out_refs)


def _conv_kernel(x_ref, xh_ref, w1_ref, b1_ref, dw_ref, dwb_ref, cg_ref, cb_ref, w2_ref, b2_ref,
                 lg_ref, lb_ref, rwt_ref, rb_ref, tri_ref, ltri_ref,
                 x1_ref, xsort_ref, lpos_ref, gate_ref, tstart_ref, tcnt_ref, tot_ref,
                 lhs_ref, h_ref, c_ref, mix_ref, act_ref, xb_ref, carry_ref):
    b = pl.program_id(0)
    s = pl.program_id(1)
    t = MIX_TILE

    lhs_ref[0:CONV_HALO, :] = xh_ref[0].astype(BF16)
    lhs_ref[CONV_HALO:, :] = x_ref[0].astype(BF16)
    lhs = lhs_ref[...]
    for cp in range(D_MODEL // MXU_DIM):
        ca = slice(cp * MXU_DIM, (cp + 1) * MXU_DIM)
        cg = slice(D_MODEL + cp * MXU_DIM, D_MODEL + (cp + 1) * MXU_DIM)
        a = jnp.dot(lhs, w1_ref[:, ca], preferred_element_type=F32) + b1_ref[:, ca]
        gt = jnp.dot(lhs, w1_ref[:, cg], preferred_element_type=F32) + b1_ref[:, cg]
        h = a * jax.nn.sigmoid(gt)
        for j in range(MXU_DIM // LANES):
            h_ref[cp * (MXU_DIM // LANES) + j] = h[:, j * LANES:(j + 1) * LANES]

    @pl.when(s == 0)
    def _():
        h_ref[:, 0:CONV_HALO, :] = jnp.zeros((N_LANE_BLOCKS, CONV_HALO, LANES), F32)

    off0 = CONV_HALO - (CONV_WIDTH - 1)

    def conv_body(c, carry):
        for i in range(t // ROW_CHUNK):
            acc = jnp.zeros((ROW_CHUNK, LANES), F32)
            for k in range(CONV_WIDTH):
                r0 = i * ROW_CHUNK + off0 + k
                acc = acc + dw_ref[c, k:k + 1, :] * h_ref[c, r0:r0 + ROW_CHUNK, :]
            c_ref[c, i * ROW_CHUNK:(i + 1) * ROW_CHUNK, :] = acc
        return carry
    lax.fori_loop(0, N_LANE_BLOCKS, conv_body, 0)

    def act_body(i, carry):
        rows = pl.ds(pl.multiple_of(i * ROW_CHUNK, ROW_CHUNK), ROW_CHUNK)
        z = jnp.concatenate([c_ref[c, rows, :] for c in range(N_LANE_BLOCKS)], axis=1)
        y = _layer_norm(z + dwb_ref[...], cg_ref[...], cb_ref[...])
        act_ref[rows, :] = (y * jax.nn.sigmoid(y)).astype(BF16)
        return carry
    lax.fori_loop(0, t // ROW_CHUNK, act_body, 0)

    mix_ref[...] = jnp.dot(act_ref[...], w2_ref[...], preferred_element_type=F32) + b2_ref[...]
    _deepnorm_route(x_ref, mix_ref, lg_ref, lb_ref, x1_ref, xb_ref, (b == 0) & (s == 0),
                    (rwt_ref, rb_ref, tri_ref, ltri_ref, carry_ref),
                    (xsort_ref, lpos_ref, gate_ref, tstart_ref, tcnt_ref))
    tot_ref[...] = carry_ref[...]


def _const_spec(shape):
    return pl.BlockSpec(shape, lambda b, s: (0,) * len(shape))


def _route_consts_specs():
    return [_const_spec((N_EXPERTS, D_MODEL)), _const_spec((N_EXPERTS, 1)),
            _const_spec((SORT_TILE, SORT_TILE)), _const_spec((N_EXPERTS, N_EXPERTS))]


def _mixer_outputs(steps):
    t = MIX_TILE
    step = lambda b, s: b * steps + s
    shapes = (jax.ShapeDtypeStruct((BATCH, SEQ, D_MODEL), F32),
              jax.ShapeDtypeStruct((N_SORT_TILES * SORT_SLOTS, D_PACK), U32),
              jax.ShapeDtypeStruct((TOP_K, N_TOK), I32),
              jax.ShapeDtypeStruct((TOP_K, N_TOK), F32),
              jax.ShapeDtypeStruct((N_SORT_TILES, N_EXPERTS, LANES), F32),
              jax.ShapeDtypeStruct((N_SORT_TILES, N_EXPERTS, LANES), F32),
              jax.ShapeDtypeStruct((N_EXPERTS, LANES), F32))
    specs = (pl.BlockSpec((1, t, D_MODEL), lambda b, s: (b, s, 0)),
             pl.BlockSpec((SORT_PER_MIX * SORT_SLOTS, D_PACK), lambda b, s: (step(b, s), 0)),
             pl.BlockSpec((TOP_K, t), lambda b, s: (0, step(b, s))),
             pl.BlockSpec((TOP_K, t), lambda b, s: (0, step(b, s))),
             pl.BlockSpec((SORT_PER_MIX, N_EXPERTS, LANES), lambda b, s: (step(b, s), 0, 0)),
             pl.BlockSpec((SORT_PER_MIX, N_EXPERTS, LANES), lambda b, s: (step(b, s), 0, 0)),
             pl.BlockSpec((N_EXPERTS, LANES), lambda b, s: (0, 0)))
    return shapes, specs


def _conv_layer(x, w1, b1, dw3, dwb, cg, cb, w2, b2, lg, lb, route_consts):
    t = MIX_TILE
    steps = SEQ // t
    halo_per_tile = t // CONV_HALO
    out_shapes, out_specs = _mixer_outputs(steps)
    in_specs = [
        pl.BlockSpec((1, t, D_MODEL), lambda b, s: (b, s, 0)),
        pl.BlockSpec((1, CONV_HALO, D_MODEL),
                     lambda b, s: (b, jnp.maximum(s * halo_per_tile - 1, 0), 0)),
        _const_spec((D_MODEL, 2 * D_MODEL)), _const_spec((1, 2 * D_MODEL)),
        _const_spec((N_LANE_BLOCKS, CONV_WIDTH, LANES)), _const_spec((1, D_MODEL)),
        _const_spec((1, D_MODEL)), _const_spec((1, D_MODEL)),
        _const_spec((D_MODEL, D_MODEL)), _const_spec((1, D_MODEL)),
        _const_spec((1, D_MODEL)), _const_spec((1, D_MODEL)),
    ] + _route_consts_specs()
    return pl.pallas_call(
        _conv_kernel,
        out_shape=out_shapes,
        grid=(BATCH, steps),
        in_specs=in_specs,
        out_specs=out_specs,
        scratch_shapes=[
            pltpu.VMEM((t + CONV_HALO, D_MODEL), BF16),
            pltpu.VMEM((N_LANE_BLOCKS, t + CONV_HALO, LANES), F32),
            pltpu.VMEM((N_LANE_BLOCKS, t, LANES), F32),
            pltpu.VMEM((t, D_MODEL), F32),
            pltpu.VMEM((t, D_MODEL), BF16),
            pltpu.VMEM((t, D_MODEL), BF16),
            pltpu.VMEM((N_EXPERTS, LANES), F32),
        ],
        compiler_params=pltpu.CompilerParams(
            dimension_semantics=("arbitrary", "arbitrary"), vmem_limit_bytes=VMEM_LIMIT),
        name="conv_mixer",
    )(x, x, w1, b1, dw3, dwb, cg, cb, w2, b2, lg, lb, *route_consts)


def _rope_block(v, cos, sin_signed, lower_half):
    partner = jnp.where(lower_half, pltpu.roll(v, LANES - HEAD_DIM // 2, 1),
                        pltpu.roll(v, HEAD_DIM // 2, 1))
    return v * cos + partner * sin_signed


def _lower_half_mask(rows):
    lane = lax.broadcasted_iota(I32, (rows, LANES), 1)
    return (lane % HEAD_DIM) < (HEAD_DIM // 2)


def _kv_kernel(x_ref, pos_ref, invf_ref, w_ref, b_ref, k_ref, v_ref, cos_ref, sin_ref):
    t = KV_TILE
    kv = jnp.dot(x_ref[...].astype(BF16), w_ref[...], preferred_element_type=F32) + b_ref[...]
    ang = pos_ref[...].astype(F32) * invf_ref[...]
    lower = _lower_half_mask(t)
    cos = jnp.cos(ang)
    sin = jnp.sin(ang)
    sin_signed = jnp.where(lower, -sin, sin)
    cos_ref[...] = cos
    sin_ref[...] = sin_signed
    for c in range(KV_DIM // LANES):
        cols = slice(c * LANES, (c + 1) * LANES)
        k_ref[:, cols] = _rope_block(kv[:, cols], cos, sin_signed, lower).astype(BF16)
    v_ref[...] = kv[:, KV_DIM:].astype(BF16)


def _shared_kv(x2d, pos_col, invf, kv_w, kv_b):
    t = KV_TILE
    return pl.pallas_call(
        _kv_kernel,
        out_shape=(jax.ShapeDtypeStruct((N_TOK, KV_DIM), BF16),
                   jax.ShapeDtypeStruct((N_TOK, KV_DIM), BF16),
                   jax.ShapeDtypeStruct((N_TOK, LANES), F32),
                   jax.ShapeDtypeStruct((N_TOK, LANES), F32)),
        grid=(N_TOK // t,),
        in_specs=[pl.BlockSpec((t, D_MODEL), lambda i: (i, 0)),
                  pl.BlockSpec((t, 1), lambda i: (i, 0)),
                  pl.BlockSpec((1, LANES), lambda i: (0, 0)),
                  pl.BlockSpec((D_MODEL, 2 * KV_DIM), lambda i: (0, 0)),
                  pl.BlockSpec((1, 2 * KV_DIM), lambda i: (0, 0))],
        out_specs=(pl.BlockSpec((t, KV_DIM), lambda i: (i, 0)),
                   pl.BlockSpec((t, KV_DIM), lambda i: (i, 0)),
                   pl.BlockSpec((t, LANES), lambda i: (i, 0)),
                   pl.BlockSpec((t, LANES), lambda i: (i, 0))),
        compiler_params=pltpu.CompilerParams(
            dimension_semantics=("arbitrary",), vmem_limit_bytes=VMEM_LIMIT),
        name="shared_kv",
    )(x2d, pos_col, invf, kv_w, kv_b)


def _attn_kernel(x_ref, kp_ref, k_ref, vp_ref, v_ref, cos_ref, sin_ref, wq_ref, bq_ref, sink_ref,
                 wo_ref, bo_ref, lg_ref, lb_ref, rwt_ref, rb_ref, tri_ref, ltri_ref,
                 x1_ref, xsort_ref, lpos_ref, gate_ref, tstart_ref, tcnt_ref, tot_ref,
                 q_ref, kf_ref, vf_ref, o_ref, mix_ref, xb_ref, carry_ref):
    b = pl.program_id(0)
    s = pl.program_id(1)
    t = MIX_TILE

    q = jnp.dot(x_ref[0].astype(BF16), wq_ref[...], preferred_element_type=F32) + bq_ref[...]
    lower = _lower_half_mask(t)
    cos = cos_ref[...]
    sin_signed = sin_ref[...]
    scale = 1.0 / math.sqrt(HEAD_DIM)
    for c in range(N_LANE_BLOCKS):
        cols = slice(c * LANES, (c + 1) * LANES)
        q_ref[:, cols] = (_rope_block(q[:, cols], cos, sin_signed, lower) * scale).astype(BF16)

    kf_ref[0:WINDOW, :] = kp_ref[...]
    kf_ref[WINDOW:, :] = k_ref[...]
    vf_ref[0:WINDOW, :] = vp_ref[...]
    vf_ref[WINDOW:, :] = v_ref[...]

    qi = lax.broadcasted_iota(I32, (WINDOW, 2 * WINDOW), 0)
    kj = lax.broadcasted_iota(I32, (WINDOW, 2 * WINDOW), 1)
    dist = qi + WINDOW - kj
    in_window = (dist >= 0) & (dist < WINDOW)

    def block_body(n, carry):
        r0 = pl.multiple_of(n * WINDOW, WINDOW)
        first_key = jnp.where((s == 0) & (n == 0), WINDOW, 0)
        mask = in_window & (kj >= first_key)
        for kh in range(N_KV_HEADS):
            k2 = kf_ref[pl.ds(r0, 2 * WINDOW), kh * HEAD_DIM:(kh + 1) * HEAD_DIM]
            v2 = vf_ref[pl.ds(r0, 2 * WINDOW), kh * HEAD_DIM:(kh + 1) * HEAD_DIM]
            outs = []
            for g in range(GQA_GROUP):
                h = kh * GQA_GROUP + g
                qh = q_ref[pl.ds(r0, WINDOW), h * HEAD_DIM:(h + 1) * HEAD_DIM]
                sc = lax.dot_general(qh, k2, (((1,), (1,)), ((), ())),
                                     preferred_element_type=F32)
                sc = jnp.where(mask, sc, NEG_INF)
                sink = sink_ref[h]
                m = jnp.maximum(jnp.max(sc, axis=-1, keepdims=True), sink)
                p = jnp.exp(sc - m)
                denom = jnp.sum(p, axis=-1, keepdims=True) + jnp.exp(sink - m)
                pv = jnp.dot(p.astype(BF16), v2, preferred_element_type=F32)
                outs.append(pv / denom)
            o_ref[pl.ds(r0, WINDOW), kh * GQA_GROUP * HEAD_DIM:(kh + 1) * GQA_GROUP * HEAD_DIM] = (
                jnp.concatenate(outs, axis=1).astype(BF16))
        return carry
    lax.fori_loop(0, t // WINDOW, block_body, 0)

    mix_ref[...] = jnp.dot(o_ref[...], wo_ref[...], preferred_element_type=F32) + bo_ref[...]
    _deepnorm_route(x_ref, mix_ref, lg_ref, lb_ref, x1_ref, xb_ref, (b == 0) & (s == 0),
                    (rwt_ref, rb_ref, tri_ref, ltri_ref, carry_ref),
                    (xsort_ref, lpos_ref, gate_ref, tstart_ref, tcnt_ref))
    tot_ref[...] = carry_ref[...]


def _attn_layer(x, k, v, cos, sin, wq, bq, sinks, wo, bo, lg, lb, route_consts):
    t = MIX_TILE
    steps = SEQ // t
    win_per_tile = t // WINDOW
    out_shapes, out_specs = _mixer_outputs(steps)

    def tile_map(b, s):
        return (b * steps + s, 0)

    def prev_window_map(b, s):
        return (b * (SEQ // WINDOW) + jnp.maximum(s * win_per_tile - 1, 0), 0)

    in_specs = [
        pl.BlockSpec((1, t, D_MODEL), lambda b, s: (b, s, 0)),
        pl.BlockSpec((WINDOW, KV_DIM), prev_window_map),
        pl.BlockSpec((t, KV_DIM), tile_map),
        pl.BlockSpec((WINDOW, KV_DIM), prev_window_map),
        pl.BlockSpec((t, KV_DIM), tile_map),
        pl.BlockSpec((t, LANES), tile_map),
        pl.BlockSpec((t, LANES), tile_map),
        _const_spec((D_MODEL, D_MODEL)), _const_spec((1, D_MODEL)),
        pl.BlockSpec(memory_space=pltpu.SMEM),
        _const_spec((D_MODEL, D_MODEL)), _const_spec((1, D_MODEL)),
        _const_spec((1, D_MODEL)), _const_spec((1, D_MODEL)),
    ] + _route_consts_specs()
    return pl.pallas_call(
        _attn_kernel,
        out_shape=out_shapes,
        grid=(BATCH, steps),
        in_specs=in_specs,
        out_specs=out_specs,
        scratch_shapes=[
            pltpu.VMEM((t, D_MODEL), BF16),
            pltpu.VMEM((t + WINDOW, KV_DIM), BF16),
            pltpu.VMEM((t + WINDOW, KV_DIM), BF16),
            pltpu.VMEM((t, D_MODEL), BF16),
            pltpu.VMEM((t, D_MODEL), F32),
            pltpu.VMEM((t, D_MODEL), BF16),
            pltpu.VMEM((N_EXPERTS, LANES), F32),
        ],
        compiler_params=pltpu.CompilerParams(
            dimension_semantics=("arbitrary", "arbitrary"), vmem_limit_bytes=VMEM_LIMIT),
        name="attn_mixer",
    )(x, k, k, v, v, cos, sin, wq, bq, sinks, wo, bo, lg, lb, *route_consts)


def _tile_rows(tcnt_ref, tile):
    used = jnp.int32(0)
    for e in range(N_EXPERTS):
        used = used + tcnt_ref[tile * N_EXPERTS + e]
    return used


def _run_slices(tstart_ref, tcnt_ref, pstart_ref, tile, e, loff):
    n = pl.multiple_of(tcnt_ref[tile * N_EXPERTS + e], RUN_ALIGN)
    local = pl.ds(pl.multiple_of(loff, RUN_ALIGN), n)
    glob = pl.ds(pl.multiple_of(pstart_ref[e] + tstart_ref[tile * N_EXPERTS + e], RUN_ALIGN), n)
    return local, glob, n


def _zero_unused_slots(fill_ref, pend_ref, xs_ref, zero_ref, zsem):
    zero_ref[...] = jnp.zeros_like(zero_ref)

    def pad_copy(e):
        n = pl.multiple_of(pend_ref[e] - fill_ref[e], RUN_ALIGN)
        dst = xs_ref.at[pl.ds(pl.multiple_of(fill_ref[e], RUN_ALIGN), n)]
        return pltpu.make_async_copy(zero_ref.at[pl.ds(0, n)], dst, zsem), n

    def tail_copy(j):
        rows = pl.ds(pl.multiple_of(j * EXPERT_BLOCK, EXPERT_BLOCK), EXPERT_BLOCK)
        return pltpu.make_async_copy(zero_ref, xs_ref.at[rows], zsem)

    first_tail = pend_ref[N_EXPERTS - 1] // EXPERT_BLOCK
    for e in range(N_EXPERTS):
        cp, n = pad_copy(e)
        pl.when(n > 0)(cp.start)
    lax.fori_loop(first_tail, N_BLOCKS, lambda j, c: (tail_copy(j).start(), c)[1], 0)
    for e in range(N_EXPERTS):
        cp, n = pad_copy(e)
        pl.when(n > 0)(cp.wait)
    lax.fori_loop(first_tail, N_BLOCKS, lambda j, c: (tail_copy(j).wait(), c)[1], 0)


def _dispatch_kernel(tstart_ref, tcnt_ref, pstart_ref, fill_ref, pend_ref, xsort_ref, xs_ref,
                     zero_ref, sem, zsem):
    i = pl.program_id(0)

    @pl.when(i == 0)
    def _():
        _zero_unused_slots(fill_ref, pend_ref, xs_ref, zero_ref, zsem)

    total = jnp.int32(0)
    for g in range(DISPATCH_GROUP):
        tile = i * DISPATCH_GROUP + g
        loff = jnp.int32(g * SORT_SLOTS)
        for e in range(N_EXPERTS):
            local, glob, n = _run_slices(tstart_ref, tcnt_ref, pstart_ref, tile, e, loff)
            cp = pltpu.make_async_copy(xsort_ref.at[local], xs_ref.at[glob], sem)
            pl.when(n > 0)(cp.start)
            loff = loff + n
        total = total + (loff - g * SORT_SLOTS)
    total = pl.multiple_of(total, RUN_ALIGN)

    @pl.when(total > 0)
    def _():
        pltpu.make_async_copy(xsort_ref.at[pl.ds(0, total)], xs_ref.at[pl.ds(0, total)], sem).wait()


def _dispatch(tstart, tcnt, pstart, fill, pend, xsort):
    rows = DISPATCH_GROUP * SORT_SLOTS
    grid_spec = pltpu.PrefetchScalarGridSpec(
        num_scalar_prefetch=5,
        grid=(N_SORT_TILES // DISPATCH_GROUP,),
        in_specs=[pl.BlockSpec((rows, D_PACK), lambda i, *_: (i, 0))],
        out_specs=pl.BlockSpec(memory_space=pl.ANY),
        scratch_shapes=[pltpu.VMEM((EXPERT_BLOCK, D_PACK), U32),
                        pltpu.SemaphoreType.DMA, pltpu.SemaphoreType.DMA],
    )
    return pl.pallas_call(
        _dispatch_kernel,
        out_shape=jax.ShapeDtypeStruct((N_SLOTS, D_PACK), U32),
        grid_spec=grid_spec,
        compiler_params=pltpu.CompilerParams(
            dimension_semantics=("arbitrary",), vmem_limit_bytes=VMEM_LIMIT),
        name="moe_dispatch",
    )(tstart, tcnt, pstart, fill, pend, xsort)


def _expert_kernel(be_ref, nb_ref, xs_ref, wg_ref, wu_ref, wd_ref, ys_ref, wg_s, wu_s, wd_s):
    i = pl.program_id(0)
    live = i < nb_ref[0]
    new_expert = (i == 0) | (be_ref[i] != be_ref[jnp.maximum(i - 1, 0)])

    @pl.when(new_expert & live)
    def _():
        wg_s[...] = wg_ref[...].astype(BF16)
        wu_s[...] = wu_ref[...].astype(BF16)
        wd_s[...] = wd_ref[...].astype(BF16)

    @pl.when(live)
    def _():
        x = _unpack_rows(xs_ref[...])
        hg = jnp.dot(x, wg_s[...], preferred_element_type=F32)
        hu = jnp.dot(x, wu_s[...], preferred_element_type=F32)
        h = (hg * jax.nn.sigmoid(hg) * hu).astype(BF16)
        y = jnp.dot(h, wd_s[...], preferred_element_type=F32)
        ys_ref[...] = _pack_rows(y.astype(BF16).astype(F32))

    @pl.when(jnp.logical_not(live))
    def _():
        ys_ref[...] = jnp.zeros_like(ys_ref)


def _experts(layer, blk_expert, n_live, xs, w_gate, w_up, w_down):
    def x_map(i, be, nb):
        return (jnp.minimum(i, jnp.maximum(nb[0] - 1, 0)), 0)

    w_map = lambda i, be, nb: (layer, be[i], 0, 0)
    grid_spec = pltpu.PrefetchScalarGridSpec(
        num_scalar_prefetch=2,
        grid=(N_BLOCKS,),
        in_specs=[
            pl.BlockSpec((EXPERT_BLOCK, D_PACK), x_map),
            pl.BlockSpec((None, None, D_MODEL, D_EXPERT), w_map),
            pl.BlockSpec((None, None, D_MODEL, D_EXPERT), w_map),
            pl.BlockSpec((None, None, D_EXPERT, D_MODEL), w_map),
        ],
        out_specs=pl.BlockSpec((EXPERT_BLOCK, D_PACK), lambda i, be, nb: (i, 0)),
        scratch_shapes=[pltpu.VMEM((D_MODEL, D_EXPERT), BF16),
                        pltpu.VMEM((D_MODEL, D_EXPERT), BF16),
                        pltpu.VMEM((D_EXPERT, D_MODEL), BF16)],
    )
    return pl.pallas_call(
        _expert_kernel,
        out_shape=jax.ShapeDtypeStruct((N_SLOTS, D_PACK), U32),
        grid_spec=grid_spec,
        compiler_params=pltpu.CompilerParams(
            dimension_semantics=("arbitrary",), vmem_limit_bytes=VMEM_LIMIT),
        name="moe_experts",
    )(blk_expert, n_live, xs, w_gate, w_up, w_down)


def _combine_kernel(tstart_ref, tcnt_ref, pstart_ref, x_ref, ys_ref, lpos_ref, gate_ref, lg_ref,
                    lb_ref, out_ref, ybuf_ref, y_ref, sems):
    t = SORT_TILE
    i = pl.program_id(0)
    slot = i % 2

    def gather(tile, dst_slot):
        loff = jnp.int32(0)
        for e in range(N_EXPERTS):
            local, glob, n = _run_slices(tstart_ref, tcnt_ref, pstart_ref, tile, e, loff)
            cp = pltpu.make_async_copy(ys_ref.at[glob], ybuf_ref.at[dst_slot, local],
                                       sems.at[dst_slot])
            pl.when(n > 0)(cp.start)
            loff = loff + n

    @pl.when(i == 0)
    def _():
        ybuf_ref[...] = jnp.zeros_like(ybuf_ref)
        gather(0, 0)

    @pl.when(i + 1 < N_SORT_TILES)
    def _():
        gather(i + 1, 1 - slot)

    used = pl.multiple_of(_tile_rows(tcnt_ref, i), RUN_ALIGN)
    pltpu.make_async_copy(ys_ref.at[pl.ds(0, used)], ybuf_ref.at[slot, pl.ds(0, used)],
                          sems.at[slot]).wait()
    live = lax.broadcasted_iota(I32, (SORT_SLOTS, 1), 0) < used
    ys = _unpack_rows(jnp.where(live, ybuf_ref[slot], jnp.uint32(0)))
    slot = lax.broadcasted_iota(I32, (t, SORT_SLOTS), 1)
    lpos = lpos_ref[...]
    gate = gate_ref[...]
    y = None
    for k in range(TOP_K):
        unsort = jnp.where(slot == lpos[:, k:k + 1], 1.0, 0.0).astype(BF16)
        yk = gate[:, k:k + 1] * jnp.dot(unsort, ys, preferred_element_type=F32)
        y = yk if y is None else y + yk
    y_ref[...] = y

    def body(i, carry):
        rows = pl.ds(pl.multiple_of(i * ROW_CHUNK, ROW_CHUNK), ROW_CHUNK)
        z = DEEPNORM_ALPHA * x_ref[rows, :] + y_ref[rows, :]
        out_ref[rows, :] = _layer_norm(z, lg_ref[...], lb_ref[...])
        return carry
    lax.fori_loop(0, t // ROW_CHUNK, body, 0)


def _combine(tstart, tcnt, pstart, x2d, ys, lpos_col, gate_col, lg, lb):
    t = SORT_TILE
    grid_spec = pltpu.PrefetchScalarGridSpec(
        num_scalar_prefetch=3,
        grid=(N_SORT_TILES,),
        in_specs=[pl.BlockSpec((t, D_MODEL), lambda i, *_: (i, 0)),
                  pl.BlockSpec(memory_space=pl.ANY),
                  pl.BlockSpec((t, TOP_K), lambda i, *_: (i, 0)),
                  pl.BlockSpec((t, TOP_K), lambda i, *_: (i, 0)),
                  pl.BlockSpec((1, D_MODEL), lambda i, *_: (0, 0)),
                  pl.BlockSpec((1, D_MODEL), lambda i, *_: (0, 0))],
        out_specs=pl.BlockSpec((t, D_MODEL), lambda i, *_: (i, 0)),
        scratch_shapes=[pltpu.VMEM((2, SORT_SLOTS, D_PACK), U32),
                        pltpu.VMEM((t, D_MODEL), F32),
                        pltpu.SemaphoreType.DMA((2,))],
    )
    return pl.pallas_call(
        _combine_kernel,
        out_shape=jax.ShapeDtypeStruct((N_TOK, D_MODEL), F32),
        grid_spec=grid_spec,
        compiler_params=pltpu.CompilerParams(
            dimension_semantics=("arbitrary",), vmem_limit_bytes=VMEM_LIMIT),
        name="moe_combine",
    )(tstart, tcnt, pstart, x2d, ys, lpos_col, gate_col, lg, lb)


def _moe(layer, x1, xsort, lpos, gates, tstart, tcnt, tot, w_gate, w_up, w_down, lg, lb):
    tot = tot[:, 0].astype(I32)
    region = ((tot + EXPERT_BLOCK - 1) // EXPERT_BLOCK) * EXPERT_BLOCK
    pend = jnp.cumsum(region).astype(I32)
    pstart = pend - region
    n_live = (pend[N_EXPERTS - 1:] // EXPERT_BLOCK).astype(I32)
    blk_start = jnp.arange(N_BLOCKS, dtype=I32) * EXPERT_BLOCK
    blk_expert = jnp.minimum(jnp.sum(blk_start[:, None] >= pend[None, :], axis=1),
                             N_EXPERTS - 1).astype(I32)
    tstart_flat = tstart[:, :, 0].astype(I32).reshape(-1)
    tcnt_flat = tcnt[:, :, 0].astype(I32).reshape(-1)

    xs = _dispatch(tstart_flat, tcnt_flat, pstart, pstart + tot, pend, xsort)
    ys = _experts(layer, blk_expert, n_live, xs, w_gate, w_up, w_down)
    out = _combine(tstart_flat, tcnt_flat, pstart, x1.reshape(N_TOK, D_MODEL), ys, lpos.T, gates.T,
                   lg, lb)
    return out.reshape(BATCH, SEQ, D_MODEL)


def kernel(x, positions, conv_w1, conv_b1, conv_dw, conv_dwb, conv_ln_g, conv_ln_b, conv_w2,
           conv_b2, kv_w, kv_b, attn_wq, attn_bq, attn_sinks, attn_wo, attn_bo, router_w,
           router_b, moe_w_gate, moe_w_up, moe_w_down, ln_g, ln_b):
    row = lambda v: v.reshape(1, -1)
    rwt = router_w.T.astype(BF16)
    rb = router_b.astype(F32).reshape(N_EXPERTS, 1)
    idx = jnp.arange(SORT_TILE, dtype=I32)
    tri = (idx[:, None] < idx[None, :]).astype(BF16)
    eidx = jnp.arange(N_EXPERTS, dtype=I32)
    ltri = (eidx[None, :] < eidx[:, None]).astype(BF16)
    route_consts = (rwt, rb, tri, ltri)
    inv_freq = ROPE_THETA ** (-jnp.arange(0, HEAD_DIM, 2, dtype=F32) / HEAD_DIM)
    invf = jnp.tile(inv_freq, LANES // (HEAD_DIM // 2)).reshape(1, LANES)
    pos_col = positions.reshape(N_TOK, 1)

    k = v = cos = sin = None
    for layer in range(DEPTH):
        lg0, lb0 = row(ln_g[layer, 0]), row(ln_b[layer, 0])
        if layer < N_CONV_LAYERS:
            i = layer
            dw3 = conv_dw[i].reshape(CONV_WIDTH, N_LANE_BLOCKS, LANES).transpose(1, 0, 2)
            outs = _conv_layer(
                x, conv_w1[i].astype(BF16), row(conv_b1[i]), dw3, row(conv_dwb[i]),
                row(conv_ln_g[i]), row(conv_ln_b[i]), conv_w2[i].astype(BF16), row(conv_b2[i]),
                lg0, lb0, route_consts)
        else:
            if layer == N_CONV_LAYERS:
                k, v, cos, sin = _shared_kv(x.reshape(N_TOK, D_MODEL), pos_col, invf,
                                            kv_w.astype(BF16), row(kv_b))
            j = layer - N_CONV_LAYERS
            outs = _attn_layer(
                x, k, v, cos, sin, attn_wq[j].astype(BF16), row(attn_bq[j]),
                attn_sinks[j].astype(F32), attn_wo[j].astype(BF16), row(attn_bo[j]),
                lg0, lb0, route_consts)
        x = _moe(layer, *outs, moe_w_gate, moe_w_up, moe_w_down,
                 row(ln_g[layer, 1]), row(ln_b[layer, 1]))
    return x
```

```python
import math

import jax
import jax.numpy as jnp
from jax import lax
from jax.experimental import pallas as pl
from jax.experimental.pallas import tpu as pltpu

D_MODEL = 1024
BATCH = 8
SEQ = 4096
DEPTH = 4
N_TOK = BATCH * SEQ
N_CONV_LAYERS = DEPTH // 2
CONV_WIDTH = 31
HEAD_DIM = 64
N_Q_HEADS = D_MODEL // HEAD_DIM
N_KV_HEADS = 4
GQA_GROUP = N_Q_HEADS // N_KV_HEADS
KV_DIM = N_KV_HEADS * HEAD_DIM
WINDOW = 128
ROPE_THETA = 10000.0
N_EXPERTS = 16
N_GROUPS = 4
EXPERTS_PER_GROUP = N_EXPERTS // N_GROUPS
TOP_K = 2
D_EXPERT = D_MODEL // 2
LN_EPS = 1e-5
NEG_INF = -1e30
DEEPNORM_ALPHA = (2.0 * DEPTH) ** 0.25

LANES = 128
SUBLANES = 8
MXU_DIM = 256
N_LANE_BLOCKS = D_MODEL // LANES
D_PACK = D_MODEL // 2

MIX_TILE = 512
CONV_HALO = 32
ROW_CHUNK = 64
KV_TILE = 1024

SORT_TILE = 256
RUN_ALIGN = SUBLANES
SORT_SLOTS = TOP_K * SORT_TILE + LANES
N_SORT_TILES = N_TOK // SORT_TILE
SORT_PER_MIX = MIX_TILE // SORT_TILE
DISPATCH_GROUP = 4
EXPERT_BLOCK = 256
_MAX_RUN_ROWS = N_TOK * TOP_K + N_SORT_TILES * N_EXPERTS * (RUN_ALIGN - 1)
N_BLOCKS = (_MAX_RUN_ROWS + N_EXPERTS * (EXPERT_BLOCK - 1)) // EXPERT_BLOCK + 1
N_SLOTS = N_BLOCKS * EXPERT_BLOCK

VMEM_LIMIT = 56 * 1024 * 1024

F32 = jnp.float32
BF16 = jnp.bfloat16
U32 = jnp.uint32
I32 = jnp.int32

assert SORT_SLOTS >= TOP_K * SORT_TILE + N_EXPERTS * (RUN_ALIGN - 1) and SORT_SLOTS % RUN_ALIGN == 0


def _layer_norm(z, g, b):
    mu = jnp.mean(z, axis=-1, keepdims=True)
    zc = z - mu
    var = jnp.mean(zc * zc, axis=-1, keepdims=True)
    return zc * lax.rsqrt(var + LN_EPS) * g + b


def _pack_rows(v):
    hi = lax.bitcast_convert_type(v[:, :D_PACK], U32) & jnp.uint32(0xFFFF0000)
    lo = lax.bitcast_convert_type(v[:, D_PACK:], U32) >> 16
    return hi | lo


def _unpack_rows(w):
    hi = lax.bitcast_convert_type(w & jnp.uint32(0xFFFF0000), F32)
    lo = lax.bitcast_convert_type(w << 16, F32)
    return jnp.concatenate([hi, lo], axis=1).astype(BF16)


def _top2_sum(a, b, c, d):
    hi1, lo1 = jnp.maximum(a, b), jnp.minimum(a, b)
    hi2, lo2 = jnp.maximum(c, d), jnp.minimum(c, d)
    top1 = jnp.maximum(hi1, hi2)
    top2 = jnp.maximum(jnp.minimum(hi1, hi2), jnp.maximum(lo1, lo2))
    return top1 + top2


def _argmax4(vals):
    best, idx = vals[0], jnp.zeros(vals[0].shape, I32)
    for j in range(1, 4):
        better = vals[j] > best
        idx = jnp.where(better, j, idx)
        best = jnp.where(better, vals[j], best)
    return idx


def _pick4(idx, vals):
    out = vals[3]
    for j in (2, 1, 0):
        out = jnp.where(idx == j, vals[j], out)
    return out


def _route_sort(half, xb, rwt_ref, rb_ref, tri_ref, ltri_ref, carry_ref,
                xsort_ref, lpos_ref, gate_ref, tstart_ref, tcnt_ref):
    t = SORT_TILE
    cols = slice(half * t, (half + 1) * t)
    logits = lax.dot_general(rwt_ref[...], xb, (((1,), (1,)), ((), ())),
                             preferred_element_type=F32)
    aff = jax.nn.sigmoid(logits)
    sel = aff + rb_ref[...]
    sel_rows = [sel[e:e + 1, :] for e in range(N_EXPERTS)]
    aff_rows = [aff[e:e + 1, :] for e in range(N_EXPERTS)]
    gscore = [_top2_sum(*sel_rows[4 * g:4 * g + 4]) for g in range(N_GROUPS)]
    grp = _argmax4(gscore)
    sel_in = [_pick4(grp, [sel_rows[4 * g + j] for g in range(N_GROUPS)]) for j in range(4)]
    aff_in = [_pick4(grp, [aff_rows[4 * g + j] for g in range(N_GROUPS)]) for j in range(4)]
    i0 = _argmax4(sel_in)
    i1 = _argmax4([jnp.where(i0 == j, -jnp.inf, sel_in[j]) for j in range(4)])
    a0 = _pick4(i0, aff_in)
    a1 = _pick4(i1, aff_in)
    gsum = a0 + a1
    gate_ref[0:1, cols] = a0 / gsum
    gate_ref[1:2, cols] = a1 / gsum
    e0 = grp * EXPERTS_PER_GROUP + i0
    e1 = grp * EXPERTS_PER_GROUP + i1

    eiota = lax.broadcasted_iota(I32, (N_EXPERTS, t), 0)
    hit0 = eiota == e0
    hit1 = eiota == e1
    onehot = jnp.where(hit0 | hit1, 1.0, 0.0)
    before = jnp.dot(onehot.astype(BF16), tri_ref[...], preferred_element_type=F32)
    cnt = jnp.sum(onehot, axis=1, keepdims=True)
    run = jnp.floor((cnt + (RUN_ALIGN - 1)) * (1.0 / RUN_ALIGN)) * RUN_ALIGN
    run_b = jnp.broadcast_to(run, (N_EXPERTS, LANES))
    loff = jnp.dot(ltri_ref[...], run_b.astype(BF16), preferred_element_type=F32)
    pos = before + loff[:, 0:1]
    lp0 = jnp.sum(jnp.where(hit0, pos, 0.0), axis=0, keepdims=True).astype(I32)
    lp1 = jnp.sum(jnp.where(hit1, pos, 0.0), axis=0, keepdims=True).astype(I32)
    lpos_ref[0:1, cols] = lp0
    lpos_ref[1:2, cols] = lp1

    slot = lax.broadcasted_iota(I32, (SORT_SLOTS, t), 0)
    perm = jnp.where((slot == lp0) | (slot == lp1), 1.0, 0.0).astype(BF16)
    rows = jnp.dot(perm, xb, preferred_element_type=F32)
    xsort_ref[half * SORT_SLOTS:(half + 1) * SORT_SLOTS, :] = _pack_rows(rows)

    tstart_ref[half] = carry_ref[...]
    tcnt_ref[half] = run_b
    carry_ref[...] = carry_ref[...] + run_b


def _deepnorm_route(x_ref, mix_ref, lg_ref, lb_ref, x1_ref, xb_ref, first, route_refs, out_refs):
    carry_ref = route_refs[-1]

    @pl.when(first)
    def _():
        carry_ref[...] = jnp.zeros_like(carry_ref)

    def body(i, c):
        rows = pl.ds(pl.multiple_of(i * ROW_CHUNK, ROW_CHUNK), ROW_CHUNK)
        z = DEEPNORM_ALPHA * x_ref[0, rows, :] + mix_ref[rows, :]
        y = _layer_norm(z, lg_ref[...], lb_ref[...])
        x1_ref[0, rows, :] = y
        xb_ref[rows, :] = y.astype(BF16)
        return c
    lax.fori_loop(0, MIX_TILE // ROW_CHUNK, body, 0, unroll=4)

    for half in range(SORT_PER_MIX):
        xb = xb_ref[half * SORT_TILE:(half + 1) * SORT_TILE, :]
        _route_sort(half, xb, *route_refs, *out_refs)


def _conv_kernel(x_ref, xh_ref, w1_ref, b1_ref, dw_ref, dwb_ref, cg_ref, cb_ref, w2_ref, b2_ref,
                 lg_ref, lb_ref, rwt_ref, rb_ref, tri_ref, ltri_ref,
                 x1_ref, xsort_ref, lpos_ref, gate_ref, tstart_ref, tcnt_ref, tot_ref,
                 lhs_ref, h_ref, c_ref, mix_ref, act_ref, xb_ref, carry_ref):
    b = pl.program_id(0)
    s = pl.program_id(1)
    t = MIX_TILE

    lhs_ref[0:CONV_HALO, :] = xh_ref[0].astype(BF16)
    lhs_ref[CONV_HALO:, :] = x_ref[0].astype(BF16)
    lhs = lhs_ref[...]
    for cp in range(D_MODEL // MXU_DIM):
        ca = slice(cp * MXU_DIM, (cp + 1) * MXU_DIM)
        cg = slice(D_MODEL + cp * MXU_DIM, D_MODEL + (cp + 1) * MXU_DIM)
        a = jnp.dot(lhs, w1_ref[:, ca], preferred_element_type=F32) + b1_ref[:, ca]
        gt = jnp.dot(lhs, w1_ref[:, cg], preferred_element_type=F32) + b1_ref[:, cg]
        h = a * jax.nn.sigmoid(gt)
        for j in range(MXU_DIM // LANES):
            h_ref[cp * (MXU_DIM // LANES) + j] = h[:, j * LANES:(j + 1) * LANES]

    @pl.when(s == 0)
    def _():
        h_ref[:, 0:CONV_HALO, :] = jnp.zeros((N_LANE_BLOCKS, CONV_HALO, LANES), F32)

    off0 = CONV_HALO - (CONV_WIDTH - 1)

    def conv_body(c, carry):
        for i in range(t // ROW_CHUNK):
            acc = jnp.zeros((ROW_CHUNK, LANES), F32)
            for k in range(CONV_WIDTH):
                r0 = i * ROW_CHUNK + off0 + k
                acc = acc + dw_ref[c, k:k + 1, :] * h_ref[c, r0:r0 + ROW_CHUNK, :]
            c_ref[c, i * ROW_CHUNK:(i + 1) * ROW_CHUNK, :] = acc
        return carry
    lax.fori_loop(0, N_LANE_BLOCKS, conv_body, 0)

    def act_body(i, carry):
        rows = pl.ds(pl.multiple_of(i * ROW_CHUNK, ROW_CHUNK), ROW_CHUNK)
        z = jnp.concatenate([c_ref[c, rows, :] for c in range(N_LANE_BLOCKS)], axis=1)
        y = _layer_norm(z + dwb_ref[...], cg_ref[...], cb_ref[...])
        act_ref[rows, :] = (y * jax.nn.sigmoid(y)).astype(BF16)
        return carry
    lax.fori_loop(0, t // ROW_CHUNK, act_body, 0, unroll=4)

    mix_ref[...] = jnp.dot(act_ref[...], w2_ref[...], preferred_element_type=F32) + b2_ref[...]
    _deepnorm_route(x_ref, mix_ref, lg_ref, lb_ref, x1_ref, xb_ref, (b == 0) & (s == 0),
                    (rwt_ref, rb_ref, tri_ref, ltri_ref, carry_ref),
                    (xsort_ref, lpos_ref, gate_ref, tstart_ref, tcnt_ref))
    tot_ref[...] = carry_ref[...]


def _const_spec(shape):
    return pl.BlockSpec(shape, lambda b, s: (0,) * len(shape))


def _route_consts_specs():
    return [_const_spec((N_EXPERTS, D_MODEL)), _const_spec((N_EXPERTS, 1)),
            _const_spec((SORT_TILE, SORT_TILE)), _const_spec((N_EXPERTS, N_EXPERTS))]


def _mixer_outputs(steps):
    t = MIX_TILE
    step = lambda b, s: b * steps + s
    shapes = (jax.ShapeDtypeStruct((BATCH, SEQ, D_MODEL), F32),
              jax.ShapeDtypeStruct((N_SORT_TILES * SORT_SLOTS, D_PACK), U32),
              jax.ShapeDtypeStruct((TOP_K, N_TOK), I32),
              jax.ShapeDtypeStruct((TOP_K, N_TOK), F32),
              jax.ShapeDtypeStruct((N_SORT_TILES, N_EXPERTS, LANES), F32),
              jax.ShapeDtypeStruct((N_SORT_TILES, N_EXPERTS, LANES), F32),
              jax.ShapeDtypeStruct((N_EXPERTS, LANES), F32))
    specs = (pl.BlockSpec((1, t, D_MODEL), lambda b, s: (b, s, 0)),
             pl.BlockSpec((SORT_PER_MIX * SORT_SLOTS, D_PACK), lambda b, s: (step(b, s), 0)),
             pl.BlockSpec((TOP_K, t), lambda b, s: (0, step(b, s))),
             pl.BlockSpec((TOP_K, t), lambda b, s: (0, step(b, s))),
             pl.BlockSpec((SORT_PER_MIX, N_EXPERTS, LANES), lambda b, s: (step(b, s), 0, 0)),
             pl.BlockSpec((SORT_PER_MIX, N_EXPERTS, LANES), lambda b, s: (step(b, s), 0, 0)),
             pl.BlockSpec((N_EXPERTS, LANES), lambda b, s: (0, 0)))
    return shapes, specs


def _conv_layer(x, w1, b1, dw3, dwb, cg, cb, w2, b2, lg, lb, route_consts):
    t = MIX_TILE
    steps = SEQ // t
    halo_per_tile = t // CONV_HALO
    out_shapes, out_specs = _mixer_outputs(steps)
    in_specs = [
        pl.BlockSpec((1, t, D_MODEL), lambda b, s: (b, s, 0)),
        pl.BlockSpec((1, CONV_HALO, D_MODEL),
                     lambda b, s: (b, jnp.maximum(s * halo_per_tile - 1, 0), 0)),
        _const_spec((D_MODEL, 2 * D_MODEL)), _const_spec((1, 2 * D_MODEL)),
        _const_spec((N_LANE_BLOCKS, CONV_WIDTH, LANES)), _const_spec((1, D_MODEL)),
        _const_spec((1, D_MODEL)), _const_spec((1, D_MODEL)),
        _const_spec((D_MODEL, D_MODEL)), _const_spec((1, D_MODEL)),
        _const_spec((1, D_MODEL)), _const_spec((1, D_MODEL)),
    ] + _route_consts_specs()
    return pl.pallas_call(
        _conv_kernel,
        out_shape=out_shapes,
        grid=(BATCH, steps),
        in_specs=in_specs,
        out_specs=out_specs,
        scratch_shapes=[
            pltpu.VMEM((t + CONV_HALO, D_MODEL), BF16),
            pltpu.VMEM((N_LANE_BLOCKS, t + CONV_HALO, LANES), F32),
            pltpu.VMEM((N_LANE_BLOCKS, t, LANES), F32),
            pltpu.VMEM((t, D_MODEL), F32),
            pltpu.VMEM((t, D_MODEL), BF16),
            pltpu.VMEM((t, D_MODEL), BF16),
            pltpu.VMEM((N_EXPERTS, LANES), F32),
        ],
        compiler_params=pltpu.CompilerParams(
            dimension_semantics=("arbitrary", "arbitrary"), vmem_limit_bytes=VMEM_LIMIT),
        name="conv_mixer",
    )(x, x, w1, b1, dw3, dwb, cg, cb, w2, b2, lg, lb, *route_consts)


def _rope_block(v, cos, sin_signed, lower_half):
    partner = jnp.where(lower_half, pltpu.roll(v, LANES - HEAD_DIM // 2, 1),
                        pltpu.roll(v, HEAD_DIM // 2, 1))
    return v * cos + partner * sin_signed


def _lower_half_mask(rows):
    lane = lax.broadcasted_iota(I32, (rows, LANES), 1)
    return (lane % HEAD_DIM) < (HEAD_DIM // 2)


def _kv_kernel(x_ref, pos_ref, invf_ref, w_ref, b_ref, k_ref, v_ref, cos_ref, sin_ref):
    t = KV_TILE
    kv = jnp.dot(x_ref[...].astype(BF16), w_ref[...], preferred_element_type=F32) + b_ref[...]
    ang = pos_ref[...].astype(F32) * invf_ref[...]
    lower = _lower_half_mask(t)
    cos = jnp.cos(ang)
    sin = jnp.sin(ang)
    sin_signed = jnp.where(lower, -sin, sin)
    cos_ref[...] = cos
    sin_ref[...] = sin_signed
    for c in range(KV_DIM // LANES):
        cols = slice(c * LANES, (c + 1) * LANES)
        k_ref[:, cols] = _rope_block(kv[:, cols], cos, sin_signed, lower).astype(BF16)
    v_ref[...] = kv[:, KV_DIM:].astype(BF16)


def _shared_kv(x2d, pos_col, invf, kv_w, kv_b):
    t = KV_TILE
    return pl.pallas_call(
        _kv_kernel,
        out_shape=(jax.ShapeDtypeStruct((N_TOK, KV_DIM), BF16),
                   jax.ShapeDtypeStruct((N_TOK, KV_DIM), BF16),
                   jax.ShapeDtypeStruct((N_TOK, LANES), F32),
                   jax.ShapeDtypeStruct((N_TOK, LANES), F32)),
        grid=(N_TOK // t,),
        in_specs=[pl.BlockSpec((t, D_MODEL), lambda i: (i, 0)),
                  pl.BlockSpec((t, 1), lambda i: (i, 0)),
                  pl.BlockSpec((1, LANES), lambda i: (0, 0)),
                  pl.BlockSpec((D_MODEL, 2 * KV_DIM), lambda i: (0, 0)),
                  pl.BlockSpec((1, 2 * KV_DIM), lambda i: (0, 0))],
        out_specs=(pl.BlockSpec((t, KV_DIM), lambda i: (i, 0)),
                   pl.BlockSpec((t, KV_DIM), lambda i: (i, 0)),
                   pl.BlockSpec((t, LANES), lambda i: (i, 0)),
                   pl.BlockSpec((t, LANES), lambda i: (i, 0))),
        compiler_params=pltpu.CompilerParams(
            dimension_semantics=("arbitrary",), vmem_limit_bytes=VMEM_LIMIT),
        name="shared_kv",
    )(x2d, pos_col, invf, kv_w, kv_b)


def _attn_kernel(x_ref, kp_ref, k_ref, vp_ref, v_ref, cos_ref, sin_ref, wq_ref, bq_ref, sink_ref,
                 wo_ref, bo_ref, lg_ref, lb_ref, rwt_ref, rb_ref, tri_ref, ltri_ref,
                 x1_ref, xsort_ref, lpos_ref, gate_ref, tstart_ref, tcnt_ref, tot_ref,
                 q_ref, kf_ref, vf_ref, o_ref, mix_ref, xb_ref, carry_ref):
    b = pl.program_id(0)
    s = pl.program_id(1)
    t = MIX_TILE

    q = jnp.dot(x_ref[0].astype(BF16), wq_ref[...], preferred_element_type=F32) + bq_ref[...]
    lower = _lower_half_mask(t)
    cos = cos_ref[...]
    sin_signed = sin_ref[...]
    scale = 1.0 / math.sqrt(HEAD_DIM)
    for c in range(N_LANE_BLOCKS):
        cols = slice(c * LANES, (c + 1) * LANES)
        q_ref[:, cols] = (_rope_block(q[:, cols], cos, sin_signed, lower) * scale).astype(BF16)

    kf_ref[0:WINDOW, :] = kp_ref[...]
    kf_ref[WINDOW:, :] = k_ref[...]
    vf_ref[0:WINDOW, :] = vp_ref[...]
    vf_ref[WINDOW:, :] = v_ref[...]

    qi = lax.broadcasted_iota(I32, (WINDOW, 2 * WINDOW), 0)
    kj = lax.broadcasted_iota(I32, (WINDOW, 2 * WINDOW), 1)
    dist = qi + WINDOW - kj
    in_window = (dist >= 0) & (dist < WINDOW)

    def block_body(n, carry):
        r0 = pl.multiple_of(n * WINDOW, WINDOW)
        first_key = jnp.where((s == 0) & (n == 0), WINDOW, 0)
        mask = in_window & (kj >= first_key)
        for kh in range(N_KV_HEADS):
            k2 = kf_ref[pl.ds(r0, 2 * WINDOW), kh * HEAD_DIM:(kh + 1) * HEAD_DIM]
            v2 = vf_ref[pl.ds(r0, 2 * WINDOW), kh * HEAD_DIM:(kh + 1) * HEAD_DIM]
            outs = []
            for g in range(GQA_GROUP):
                h = kh * GQA_GROUP + g
                qh = q_ref[pl.ds(r0, WINDOW), h * HEAD_DIM:(h + 1) * HEAD_DIM]
                sc = lax.dot_general(qh, k2, (((1,), (1,)), ((), ())),
                                     preferred_element_type=F32)
                sc = jnp.where(mask, sc, NEG_INF)
                sink = sink_ref[h]
                m = jnp.maximum(jnp.max(sc, axis=-1, keepdims=True), sink)
                p = jnp.exp(sc - m)
                denom = jnp.sum(p, axis=-1, keepdims=True) + jnp.exp(sink - m)
                pv = jnp.dot(p.astype(BF16), v2, preferred_element_type=F32)
                outs.append(pv / denom)
            o_ref[pl.ds(r0, WINDOW), kh * GQA_GROUP * HEAD_DIM:(kh + 1) * GQA_GROUP * HEAD_DIM] = (
                jnp.concatenate(outs, axis=1).astype(BF16))
        return carry
    lax.fori_loop(0, t // WINDOW, block_body, 0)

    mix_ref[...] = jnp.dot(o_ref[...], wo_ref[...], preferred_element_type=F32) + bo_ref[...]
    _deepnorm_route(x_ref, mix_ref, lg_ref, lb_ref, x1_ref, xb_ref, (b == 0) & (s == 0),
                    (rwt_ref, rb_ref, tri_ref, ltri_ref, carry_ref),
                    (xsort_ref, lpos_ref, gate_ref, tstart_ref, tcnt_ref))
    tot_ref[...] = carry_ref[...]


def _attn_layer(x, k, v, cos, sin, wq, bq, sinks, wo, bo, lg, lb, route_consts):
    t = MIX_TILE
    steps = SEQ // t
    win_per_tile = t // WINDOW
    out_shapes, out_specs = _mixer_outputs(steps)

    def tile_map(b, s):
        return (b * steps + s, 0)

    def prev_window_map(b, s):
        return (b * (SEQ // WINDOW) + jnp.maximum(s * win_per_tile - 1, 0), 0)

    in_specs = [
        pl.BlockSpec((1, t, D_MODEL), lambda b, s: (b, s, 0)),
        pl.BlockSpec((WINDOW, KV_DIM), prev_window_map),
        pl.BlockSpec((t, KV_DIM), tile_map),
        pl.BlockSpec((WINDOW, KV_DIM), prev_window_map),
        pl.BlockSpec((t, KV_DIM), tile_map),
        pl.BlockSpec((t, LANES), tile_map),
        pl.BlockSpec((t, LANES), tile_map),
        _const_spec((D_MODEL, D_MODEL)), _const_spec((1, D_MODEL)),
        pl.BlockSpec(memory_space=pltpu.SMEM),
        _const_spec((D_MODEL, D_MODEL)), _const_spec((1, D_MODEL)),
        _const_spec((1, D_MODEL)), _const_spec((1, D_MODEL)),
    ] + _route_consts_specs()
    return pl.pallas_call(
        _attn_kernel,
        out_shape=out_shapes,
        grid=(BATCH, steps),
        in_specs=in_specs,
        out_specs=out_specs,
        scratch_shapes=[
            pltpu.VMEM((t, D_MODEL), BF16),
            pltpu.VMEM((t + WINDOW, KV_DIM), BF16),
            pltpu.VMEM((t + WINDOW, KV_DIM), BF16),
            pltpu.VMEM((t, D_MODEL), BF16),
            pltpu.VMEM((t, D_MODEL), F32),
            pltpu.VMEM((t, D_MODEL), BF16),
            pltpu.VMEM((N_EXPERTS, LANES), F32),
        ],
        compiler_params=pltpu.CompilerParams(
            dimension_semantics=("arbitrary", "arbitrary"), vmem_limit_bytes=VMEM_LIMIT),
        name="attn_mixer",
    )(x, k, k, v, v, cos, sin, wq, bq, sinks, wo, bo, lg, lb, *route_consts)


def _tile_rows(tcnt_ref, tile):
    used = jnp.int32(0)
    for e in range(N_EXPERTS):
        used = used + tcnt_ref[tile * N_EXPERTS + e]
    return used


def _run_slices(tstart_ref, tcnt_ref, pstart_ref, tile, e, loff):
    n = pl.multiple_of(tcnt_ref[tile * N_EXPERTS + e], RUN_ALIGN)
    local = pl.ds(pl.multiple_of(loff, RUN_ALIGN), n)
    glob = pl.ds(pl.multiple_of(pstart_ref[e] + tstart_ref[tile * N_EXPERTS + e], RUN_ALIGN), n)
    return local, glob, n


def _zero_unused_slots(fill_ref, pend_ref, xs_ref, zero_ref, zsem):
    zero_ref[...] = jnp.zeros_like(zero_ref)

    def pad_copy(e):
        n = pl.multiple_of(pend_ref[e] - fill_ref[e], RUN_ALIGN)
        dst = xs_ref.at[pl.ds(pl.multiple_of(fill_ref[e], RUN_ALIGN), n)]
        return pltpu.make_async_copy(zero_ref.at[pl.ds(0, n)], dst, zsem), n

    def tail_copy(j):
        rows = pl.ds(pl.multiple_of(j * EXPERT_BLOCK, EXPERT_BLOCK), EXPERT_BLOCK)
        return pltpu.make_async_copy(zero_ref, xs_ref.at[rows], zsem)

    first_tail = pend_ref[N_EXPERTS - 1] // EXPERT_BLOCK
    for e in range(N_EXPERTS):
        cp, n = pad_copy(e)
        pl.when(n > 0)(cp.start)
    lax.fori_loop(first_tail, N_BLOCKS, lambda j, c: (tail_copy(j).start(), c)[1], 0)
    for e in range(N_EXPERTS):
        cp, n = pad_copy(e)
        pl.when(n > 0)(cp.wait)
    lax.fori_loop(first_tail, N_BLOCKS, lambda j, c: (tail_copy(j).wait(), c)[1], 0)


def _dispatch_kernel(tstart_ref, tcnt_ref, pstart_ref, fill_ref, pend_ref, xsort_ref, xs_ref,
                     zero_ref, sem, zsem):
    i = pl.program_id(0)

    @pl.when(i == 0)
    def _():
        _zero_unused_slots(fill_ref, pend_ref, xs_ref, zero_ref, zsem)

    total = jnp.int32(0)
    for g in range(DISPATCH_GROUP):
        tile = i * DISPATCH_GROUP + g
        loff = jnp.int32(g * SORT_SLOTS)
        for e in range(N_EXPERTS):
            local, glob, n = _run_slices(tstart_ref, tcnt_ref, pstart_ref, tile, e, loff)
            cp = pltpu.make_async_copy(xsort_ref.at[local], xs_ref.at[glob], sem)
            pl.when(n > 0)(cp.start)
            loff = loff + n
        total = total + (loff - g * SORT_SLOTS)
    total = pl.multiple_of(total, RUN_ALIGN)

    @pl.when(total > 0)
    def _():
        pltpu.make_async_copy(xsort_ref.at[pl.ds(0, total)], xs_ref.at[pl.ds(0, total)], sem).wait()


def _dispatch(tstart, tcnt, pstart, fill, pend, xsort):
    rows = DISPATCH_GROUP * SORT_SLOTS
    grid_spec = pltpu.PrefetchScalarGridSpec(
        num_scalar_prefetch=5,
        grid=(N_SORT_TILES // DISPATCH_GROUP,),
        in_specs=[pl.BlockSpec((rows, D_PACK), lambda i, *_: (i, 0))],
        out_specs=pl.BlockSpec(memory_space=pl.ANY),
        scratch_shapes=[pltpu.VMEM((EXPERT_BLOCK, D_PACK), U32),
                        pltpu.SemaphoreType.DMA, pltpu.SemaphoreType.DMA],
    )
    return pl.pallas_call(
        _dispatch_kernel,
        out_shape=jax.ShapeDtypeStruct((N_SLOTS, D_PACK), U32),
        grid_spec=grid_spec,
        compiler_params=pltpu.CompilerParams(
            dimension_semantics=("arbitrary",), vmem_limit_bytes=VMEM_LIMIT),
        name="moe_dispatch",
    )(tstart, tcnt, pstart, fill, pend, xsort)


def _expert_kernel(be_ref, nb_ref, xs_ref, wg_ref, wu_ref, wd_ref, ys_ref, wg_s, wu_s, wd_s):
    i = pl.program_id(0)
    live = i < nb_ref[0]
    new_expert = (i == 0) | (be_ref[i] != be_ref[jnp.maximum(i - 1, 0)])

    @pl.when(new_expert & live)
    def _():
        wg_s[...] = wg_ref[...].astype(BF16)
        wu_s[...] = wu_ref[...].astype(BF16)
        wd_s[...] = wd_ref[...].astype(BF16)

    @pl.when(live)
    def _():
        x = _unpack_rows(xs_ref[...])
        hg = jnp.dot(x, wg_s[...], preferred_element_type=F32)
        hu = jnp.dot(x, wu_s[...], preferred_element_type=F32)
        h = (hg * jax.nn.sigmoid(hg) * hu).astype(BF16)
        y = jnp.dot(h, wd_s[...], preferred_element_type=F32)
        ys_ref[...] = _pack_rows(y.astype(BF16).astype(F32))

    @pl.when(jnp.logical_not(live))
    def _():
        ys_ref[...] = jnp.zeros_like(ys_ref)


def _experts(layer, blk_expert, n_live, xs, w_gate, w_up, w_down):
    def x_map(i, be, nb):
        return (jnp.minimum(i, jnp.maximum(nb[0] - 1, 0)), 0)

    w_map = lambda i, be, nb: (layer, be[i], 0, 0)
    grid_spec = pltpu.PrefetchScalarGridSpec(
        num_scalar_prefetch=2,
        grid=(N_BLOCKS,),
        in_specs=[
            pl.BlockSpec((EXPERT_BLOCK, D_PACK), x_map),
            pl.BlockSpec((None, None, D_MODEL, D_EXPERT), w_map),
            pl.BlockSpec((None, None, D_MODEL, D_EXPERT), w_map),
            pl.BlockSpec((None, None, D_EXPERT, D_MODEL), w_map),
        ],
        out_specs=pl.BlockSpec((EXPERT_BLOCK, D_PACK), lambda i, be, nb: (i, 0)),
        scratch_shapes=[pltpu.VMEM((D_MODEL, D_EXPERT), BF16),
                        pltpu.VMEM((D_MODEL, D_EXPERT), BF16),
                        pltpu.VMEM((D_EXPERT, D_MODEL), BF16)],
    )
    return pl.pallas_call(
        _expert_kernel,
        out_shape=jax.ShapeDtypeStruct((N_SLOTS, D_PACK), U32),
        grid_spec=grid_spec,
        compiler_params=pltpu.CompilerParams(
            dimension_semantics=("arbitrary",), vmem_limit_bytes=VMEM_LIMIT),
        name="moe_experts",
    )(blk_expert, n_live, xs, w_gate, w_up, w_down)


def _combine_kernel(tstart_ref, tcnt_ref, pstart_ref, x_ref, ys_ref, lpos_ref, gate_ref, lg_ref,
                    lb_ref, out_ref, ybuf_ref, y_ref, sems):
    t = SORT_TILE
    i = pl.program_id(0)
    slot = i % 2

    def gather(tile, dst_slot):
        loff = jnp.int32(0)
        for e in range(N_EXPERTS):
            local, glob, n = _run_slices(tstart_ref, tcnt_ref, pstart_ref, tile, e, loff)
            cp = pltpu.make_async_copy(ys_ref.at[glob], ybuf_ref.at[dst_slot, local],
                                       sems.at[dst_slot])
            pl.when(n > 0)(cp.start)
            loff = loff + n

    @pl.when(i == 0)
    def _():
        ybuf_ref[...] = jnp.zeros_like(ybuf_ref)
        gather(0, 0)

    @pl.when(i + 1 < N_SORT_TILES)
    def _():
        gather(i + 1, 1 - slot)

    used = pl.multiple_of(_tile_rows(tcnt_ref, i), RUN_ALIGN)
    pltpu.make_async_copy(ys_ref.at[pl.ds(0, used)], ybuf_ref.at[slot, pl.ds(0, used)],
                          sems.at[slot]).wait()
    live = lax.broadcasted_iota(I32, (SORT_SLOTS, 1), 0) < used
    ys = _unpack_rows(jnp.where(live, ybuf_ref[slot], jnp.uint32(0)))
    slot = lax.broadcasted_iota(I32, (t, SORT_SLOTS), 1)
    lpos = lpos_ref[...]
    gate = gate_ref[...]
    y = None
    for k in range(TOP_K):
        unsort = jnp.where(slot == lpos[:, k:k + 1], 1.0, 0.0).astype(BF16)
        yk = gate[:, k:k + 1] * jnp.dot(unsort, ys, preferred_element_type=F32)
        y = yk if y is None else y + yk
    y_ref[...] = y

    def body(i, carry):
        rows = pl.ds(pl.multiple_of(i * ROW_CHUNK, ROW_CHUNK), ROW_CHUNK)
        z = DEEPNORM_ALPHA * x_ref[rows, :] + y_ref[rows, :]
        out_ref[rows, :] = _layer_norm(z, lg_ref[...], lb_ref[...])
        return carry
    lax.fori_loop(0, t // ROW_CHUNK, body, 0, unroll=4)


def _combine(tstart, tcnt, pstart, x2d, ys, lpos_col, gate_col, lg, lb):
    t = SORT_TILE
    grid_spec = pltpu.PrefetchScalarGridSpec(
        num_scalar_prefetch=3,
        grid=(N_SORT_TILES,),
        in_specs=[pl.BlockSpec((t, D_MODEL), lambda i, *_: (i, 0)),
                  pl.BlockSpec(memory_space=pl.ANY),
                  pl.BlockSpec((t, TOP_K), lambda i, *_: (i, 0)),
                  pl.BlockSpec((t, TOP_K), lambda i, *_: (i, 0)),
                  pl.BlockSpec((1, D_MODEL), lambda i, *_: (0, 0)),
                  pl.BlockSpec((1, D_MODEL), lambda i, *_: (0, 0))],
        out_specs=pl.BlockSpec((t, D_MODEL), lambda i, *_: (i, 0)),
        scratch_shapes=[pltpu.VMEM((2, SORT_SLOTS, D_PACK), U32),
                        pltpu.VMEM((t, D_MODEL), F32),
                        pltpu.SemaphoreType.DMA((2,))],
    )
    return pl.pallas_call(
        _combine_kernel,
        out_shape=jax.ShapeDtypeStruct((N_TOK, D_MODEL), F32),
        grid_spec=grid_spec,
        compiler_params=pltpu.CompilerParams(
            dimension_semantics=("arbitrary",), vmem_limit_bytes=VMEM_LIMIT),
        name="moe_combine",
    )(tstart, tcnt, pstart, x2d, ys, lpos_col, gate_col, lg, lb)


def _moe(layer, x1, xsort, lpos, gates, tstart, tcnt, tot, w_gate, w_up, w_down, lg, lb):
    tot = tot[:, 0].astype(I32)
    region = ((tot + EXPERT_BLOCK - 1) // EXPERT_BLOCK) * EXPERT_BLOCK
    pend = jnp.cumsum(region).astype(I32)
    pstart = pend - region
    n_live = (pend[N_EXPERTS - 1:] // EXPERT_BLOCK).astype(I32)
    blk_start = jnp.arange(N_BLOCKS, dtype=I32) * EXPERT_BLOCK
    blk_expert = jnp.minimum(jnp.sum(blk_start[:, None] >= pend[None, :], axis=1),
                             N_EXPERTS - 1).astype(I32)
    tstart_flat = tstart[:, :, 0].astype(I32).reshape(-1)
    tcnt_flat = tcnt[:, :, 0].astype(I32).reshape(-1)

    xs = _dispatch(tstart_flat, tcnt_flat, pstart, pstart + tot, pend, xsort)
    ys = _experts(layer, blk_expert, n_live, xs, w_gate, w_up, w_down)
    out = _combine(tstart_flat, tcnt_flat, pstart, x1.reshape(N_TOK, D_MODEL), ys, lpos.T, gates.T,
                   lg, lb)
    return out.reshape(BATCH, SEQ, D_MODEL)


def kernel(x, positions, conv_w1, conv_b1, conv_dw, conv_dwb, conv_ln_g, conv_ln_b, conv_w2,
           conv_b2, kv_w, kv_b, attn_wq, attn_bq, attn_sinks, attn_wo, attn_bo, router_w,
           router_b, moe_w_gate, moe_w_up, moe_w_down, ln_g, ln_b):
    row = lambda v: v.reshape(1, -1)
    rwt = router_w.T.astype(BF16)
    rb = router_b.astype(F32).reshape(N_EXPERTS, 1)
    idx = jnp.arange(SORT_TILE, dtype=I32)
    tri = (idx[:, None] < idx[None, :]).astype(BF16)
    eidx = jnp.arange(N_EXPERTS, dtype=I32)
    ltri = (eidx[None, :] < eidx[:, None]).astype(BF16)
    route_consts = (rwt, rb, tri, ltri)
    inv_freq = ROPE_THETA ** (-jnp.arange(0, HEAD_DIM, 2, dtype=F32) / HEAD_DIM)
    invf = jnp.tile(inv_freq, LANES // (HEAD_DIM // 2)).reshape(1, LANES)
    pos_col = positions.reshape(N_TOK, 1)

    k = v = cos = sin = None
    for layer in range(DEPTH):
        lg0, lb0 = row(ln_g[layer, 0]), row(ln_b[layer, 0])
        if layer < N_CONV_LAYERS:
            i = layer
            dw3 = conv_dw[i].reshape(CONV_WIDTH, N_LANE_BLOCKS, LANES).transpose(1, 0, 2)
            outs = _conv_layer(
                x, conv_w1[i].astype(BF16), row(conv_b1[i]), dw3, row(conv_dwb[i]),
                row(conv_ln_g[i]), row(conv_ln_b[i]), conv_w2[i].astype(BF16), row(conv_b2[i]),
                lg0, lb0, route_consts)
        else:
            if layer == N_CONV_LAYERS:
                k, v, cos, sin = _shared_kv(x.reshape(N_TOK, D_MODEL), pos_col, invf,
                                            kv_w.astype(BF16), row(kv_b))
            j = layer - N_CONV_LAYERS
            outs = _attn_layer(
                x, k, v, cos, sin, attn_wq[j].astype(BF16), row(attn_bq[j]),
                attn_sinks[j].astype(F32), attn_wo[j].astype(BF16), row(attn_bo[j]),
                lg0, lb0, route_consts)
        x = _moe(layer, *outs, moe_w_gate, moe_w_up, moe_w_down,
                 row(ln_g[layer, 1]), row(ln_b[layer, 1]))
    return x
```

```python
import math

import jax
import jax.numpy as jnp
from jax import lax
from jax.experimental import pallas as pl
from jax.experimental.pallas import tpu as pltpu

D_MODEL = 1024
BATCH = 8
SEQ = 4096
DEPTH = 4
N_TOK = BATCH * SEQ
N_CONV_LAYERS = DEPTH // 2
CONV_WIDTH = 31
HEAD_DIM = 64
N_Q_HEADS = D_MODEL // HEAD_DIM
N_KV_HEADS = 4
GQA_GROUP = N_Q_HEADS // N_KV_HEADS
KV_DIM = N_KV_HEADS * HEAD_DIM
WINDOW = 128
ROPE_THETA = 10000.0
N_EXPERTS = 16
N_GROUPS = 4
EXPERTS_PER_GROUP = N_EXPERTS // N_GROUPS
TOP_K = 2
D_EXPERT = D_MODEL // 2
LN_EPS = 1e-5
NEG_INF = -1e30
DEEPNORM_ALPHA = (2.0 * DEPTH) ** 0.25

LANES = 128
SUBLANES = 8
MXU_DIM = 256
N_LANE_BLOCKS = D_MODEL // LANES
D_PACK = D_MODEL // 2

MIX_TILE = 512
CONV_HALO = 32
ROW_CHUNK = 64
KV_TILE = 1024

SORT_TILE = 256
RUN_ALIGN = SUBLANES
SORT_SLOTS = TOP_K * SORT_TILE + LANES
N_SORT_TILES = N_TOK // SORT_TILE
SORT_PER_MIX = MIX_TILE // SORT_TILE
DISPATCH_GROUP = 4
EXPERT_BLOCK = 512
_MAX_RUN_ROWS = N_TOK * TOP_K + N_SORT_TILES * N_EXPERTS * (RUN_ALIGN - 1)
N_BLOCKS = (_MAX_RUN_ROWS + N_EXPERTS * (EXPERT_BLOCK - 1)) // EXPERT_BLOCK + 1
N_SLOTS = N_BLOCKS * EXPERT_BLOCK

VMEM_LIMIT = 56 * 1024 * 1024

F32 = jnp.float32
BF16 = jnp.bfloat16
U32 = jnp.uint32
I32 = jnp.int32

assert SORT_SLOTS >= TOP_K * SORT_TILE + N_EXPERTS * (RUN_ALIGN - 1) and SORT_SLOTS % RUN_ALIGN == 0


def _layer_norm(z, g, b):
    mu = jnp.mean(z, axis=-1, keepdims=True)
    zc = z - mu
    var = jnp.mean(zc * zc, axis=-1, keepdims=True)
    return zc * lax.rsqrt(var + LN_EPS) * g + b


def _pack_rows(v):
    hi = lax.bitcast_convert_type(v[:, :D_PACK], U32) & jnp.uint32(0xFFFF0000)
    lo = lax.bitcast_convert_type(v[:, D_PACK:], U32) >> 16
    return hi | lo


def _unpack_rows(w):
    hi = lax.bitcast_convert_type(w & jnp.uint32(0xFFFF0000), F32)
    lo = lax.bitcast_convert_type(w << 16, F32)
    return jnp.concatenate([hi, lo], axis=1).astype(BF16)


def _top2_sum(a, b, c, d):
    hi1, lo1 = jnp.maximum(a, b), jnp.minimum(a, b)
    hi2, lo2 = jnp.maximum(c, d), jnp.minimum(c, d)
    top1 = jnp.maximum(hi1, hi2)
    top2 = jnp.maximum(jnp.minimum(hi1, hi2), jnp.maximum(lo1, lo2))
    return top1 + top2


def _argmax4(vals):
    best, idx = vals[0], jnp.zeros(vals[0].shape, I32)
    for j in range(1, 4):
        better = vals[j] > best
        idx = jnp.where(better, j, idx)
        best = jnp.where(better, vals[j], best)
    return idx


def _pick4(idx, vals):
    out = vals[3]
    for j in (2, 1, 0):
        out = jnp.where(idx == j, vals[j], out)
    return out


def _route_sort(half, xb, rwt_ref, rb_ref, tri_ref, ltri_ref, carry_ref,
                xsort_ref, lpos_ref, gate_ref, tstart_ref, tcnt_ref):
    t = SORT_TILE
    cols = slice(half * t, (half + 1) * t)
    logits = lax.dot_general(rwt_ref[...], xb, (((1,), (1,)), ((), ())),
                             preferred_element_type=F32)
    aff = jax.nn.sigmoid(logits)
    sel = aff + rb_ref[...]
    sel_rows = [sel[e:e + 1, :] for e in range(N_EXPERTS)]
    aff_rows = [aff[e:e + 1, :] for e in range(N_EXPERTS)]
    gscore = [_top2_sum(*sel_rows[4 * g:4 * g + 4]) for g in range(N_GROUPS)]
    grp = _argmax4(gscore)
    sel_in = [_pick4(grp, [sel_rows[4 * g + j] for g in range(N_GROUPS)]) for j in range(4)]
    aff_in = [_pick4(grp, [aff_rows[4 * g + j] for g in range(N_GROUPS)]) for j in range(4)]
    i0 = _argmax4(sel_in)
    i1 = _argmax4([jnp.where(i0 == j, -jnp.inf, sel_in[j]) for j in range(4)])
    a0 = _pick4(i0, aff_in)
    a1 = _pick4(i1, aff_in)
    gsum = a0 + a1
    gate_ref[0:1, cols] = a0 / gsum
    gate_ref[1:2, cols] = a1 / gsum
    e0 = grp * EXPERTS_PER_GROUP + i0
    e1 = grp * EXPERTS_PER_GROUP + i1

    eiota = lax.broadcasted_iota(I32, (N_EXPERTS, t), 0)
    hit0 = eiota == e0
    hit1 = eiota == e1
    onehot = jnp.where(hit0 | hit1, 1.0, 0.0)
    before = jnp.dot(onehot.astype(BF16), tri_ref[...], preferred_element_type=F32)
    cnt = jnp.sum(onehot, axis=1, keepdims=True)
    run = jnp.floor((cnt + (RUN_ALIGN - 1)) * (1.0 / RUN_ALIGN)) * RUN_ALIGN
    run_b = jnp.broadcast_to(run, (N_EXPERTS, LANES))
    loff = jnp.dot(ltri_ref[...], run_b.astype(BF16), preferred_element_type=F32)
    pos = before + loff[:, 0:1]
    lp0 = jnp.sum(jnp.where(hit0, pos, 0.0), axis=0, keepdims=True).astype(I32)
    lp1 = jnp.sum(jnp.where(hit1, pos, 0.0), axis=0, keepdims=True).astype(I32)
    lpos_ref[0:1, cols] = lp0
    lpos_ref[1:2, cols] = lp1

    slot = lax.broadcasted_iota(I32, (SORT_SLOTS, t), 0)
    perm = jnp.where((slot == lp0) | (slot == lp1), 1.0, 0.0).astype(BF16)
    rows = jnp.dot(perm, xb, preferred_element_type=F32)
    xsort_ref[half * SORT_SLOTS:(half + 1) * SORT_SLOTS, :] = _pack_rows(rows)

    tstart_ref[half] = carry_ref[...]
    tcnt_ref[half] = run_b
    carry_ref[...] = carry_ref[...] + run_b


def _deepnorm_route(x_ref, mix_ref, lg_ref, lb_ref, x1_ref, xb_ref, first, route_refs, out_refs):
    carry_ref = route_refs[-1]

    @pl.when(first)
    def _():
        carry_ref[...] = jnp.zeros_like(carry_ref)

    def body(i, c):
        rows = pl.ds(pl.multiple_of(i * ROW_CHUNK, ROW_CHUNK), ROW_CHUNK)
        z = DEEPNORM_ALPHA * x_ref[0, rows, :] + mix_ref[rows, :]
        y = _layer_norm(z, lg_ref[...], lb_ref[...])
        x1_ref[0, rows, :] = y
        xb_ref[rows, :] = y.astype(BF16)
        return c
    lax.fori_loop(0, MIX_TILE // ROW_CHUNK, body, 0, unroll=4)

    for half in range(SORT_PER_MIX):
        xb = xb_ref[half * SORT_TILE:(half + 1) * SORT_TILE, :]
        _route_sort(half, xb, *route_refs, *out_refs)


def _conv_kernel(x_ref, xh_ref, w1_ref, b1_ref, dw_ref, dwb_ref, cg_ref, cb_ref, w2_ref, b2_ref,
                 lg_ref, lb_ref, rwt_ref, rb_ref, tri_ref, ltri_ref,
                 x1_ref, xsort_ref, lpos_ref, gate_ref, tstart_ref, tcnt_ref, tot_ref,
                 lhs_ref, h_ref, c_ref, mix_ref, act_ref, xb_ref, carry_ref):
    b = pl.program_id(0)
    s = pl.program_id(1)
    t = MIX_TILE

    lhs_ref[0:CONV_HALO, :] = xh_ref[0].astype(BF16)
    lhs_ref[CONV_HALO:, :] = x_ref[0].astype(BF16)
    lhs = lhs_ref[...]
    for cp in range(D_MODEL // MXU_DIM):
        ca = slice(cp * MXU_DIM, (cp + 1) * MXU_DIM)
        cg = slice(D_MODEL + cp * MXU_DIM, D_MODEL + (cp + 1) * MXU_DIM)
        a = jnp.dot(lhs, w1_ref[:, ca], preferred_element_type=F32) + b1_ref[:, ca]
        gt = jnp.dot(lhs, w1_ref[:, cg], preferred_element_type=F32) + b1_ref[:, cg]
        h = a * jax.nn.sigmoid(gt)
        for j in range(MXU_DIM // LANES):
            h_ref[cp * (MXU_DIM // LANES) + j] = h[:, j * LANES:(j + 1) * LANES]

    @pl.when(s == 0)
    def _():
        h_ref[:, 0:CONV_HALO, :] = jnp.zeros((N_LANE_BLOCKS, CONV_HALO, LANES), F32)

    off0 = CONV_HALO - (CONV_WIDTH - 1)

    def conv_body(c, carry):
        for i in range(t // ROW_CHUNK):
            acc = jnp.zeros((ROW_CHUNK, LANES), F32)
            for k in range(CONV_WIDTH):
                r0 = i * ROW_CHUNK + off0 + k
                acc = acc + dw_ref[c, k:k + 1, :] * h_ref[c, r0:r0 + ROW_CHUNK, :]
            c_ref[c, i * ROW_CHUNK:(i + 1) * ROW_CHUNK, :] = acc
        return carry
    lax.fori_loop(0, N_LANE_BLOCKS, conv_body, 0)

    def act_body(i, carry):
        rows = pl.ds(pl.multiple_of(i * ROW_CHUNK, ROW_CHUNK), ROW_CHUNK)
        z = jnp.concatenate([c_ref[c, rows, :] for c in range(N_LANE_BLOCKS)], axis=1)
        y = _layer_norm(z + dwb_ref[...], cg_ref[...], cb_ref[...])
        act_ref[rows, :] = (y * jax.nn.sigmoid(y)).astype(BF16)
        return carry
    lax.fori_loop(0, t // ROW_CHUNK, act_body, 0, unroll=4)

    mix_ref[...] = jnp.dot(act_ref[...], w2_ref[...], preferred_element_type=F32) + b2_ref[...]
    _deepnorm_route(x_ref, mix_ref, lg_ref, lb_ref, x1_ref, xb_ref, (b == 0) & (s == 0),
                    (rwt_ref, rb_ref, tri_ref, ltri_ref, carry_ref),
                    (xsort_ref, lpos_ref, gate_ref, tstart_ref, tcnt_ref))
    tot_ref[...] = carry_ref[...]


def _const_spec(shape):
    return pl.BlockSpec(shape, lambda b, s: (0,) * len(shape))


def _route_consts_specs():
    return [_const_spec((N_EXPERTS, D_MODEL)), _const_spec((N_EXPERTS, 1)),
            _const_spec((SORT_TILE, SORT_TILE)), _const_spec((N_EXPERTS, N_EXPERTS))]


def _mixer_outputs(steps):
    t = MIX_TILE
    step = lambda b, s: b * steps + s
    shapes = (jax.ShapeDtypeStruct((BATCH, SEQ, D_MODEL), F32),
              jax.ShapeDtypeStruct((N_SORT_TILES * SORT_SLOTS, D_PACK), U32),
              jax.ShapeDtypeStruct((TOP_K, N_TOK), I32),
              jax.ShapeDtypeStruct((TOP_K, N_TOK), F32),
              jax.ShapeDtypeStruct((N_SORT_TILES, N_EXPERTS, LANES), F32),
              jax.ShapeDtypeStruct((N_SORT_TILES, N_EXPERTS, LANES), F32),
              jax.ShapeDtypeStruct((N_EXPERTS, LANES), F32))
    specs = (pl.BlockSpec((1, t, D_MODEL), lambda b, s: (b, s, 0)),
             pl.BlockSpec((SORT_PER_MIX * SORT_SLOTS, D_PACK), lambda b, s: (step(b, s), 0)),
             pl.BlockSpec((TOP_K, t), lambda b, s: (0, step(b, s))),
             pl.BlockSpec((TOP_K, t), lambda b, s: (0, step(b, s))),
             pl.BlockSpec((SORT_PER_MIX, N_EXPERTS, LANES), lambda b, s: (step(b, s), 0, 0)),
             pl.BlockSpec((SORT_PER_MIX, N_EXPERTS, LANES), lambda b, s: (step(b, s), 0, 0)),
             pl.BlockSpec((N_EXPERTS, LANES), lambda b, s: (0, 0)))
    return shapes, specs


def _conv_layer(x, w1, b1, dw3, dwb, cg, cb, w2, b2, lg, lb, route_consts):
    t = MIX_TILE
    steps = SEQ // t
    halo_per_tile = t // CONV_HALO
    out_shapes, out_specs = _mixer_outputs(steps)
    in_specs = [
        pl.BlockSpec((1, t, D_MODEL), lambda b, s: (b, s, 0)),
        pl.BlockSpec((1, CONV_HALO, D_MODEL),
                     lambda b, s: (b, jnp.maximum(s * halo_per_tile - 1, 0), 0)),
        _const_spec((D_MODEL, 2 * D_MODEL)), _const_spec((1, 2 * D_MODEL)),
        _const_spec((N_LANE_BLOCKS, CONV_WIDTH, LANES)), _const_spec((1, D_MODEL)),
        _const_spec((1, D_MODEL)), _const_spec((1, D_MODEL)),
        _const_spec((D_MODEL, D_MODEL)), _const_spec((1, D_MODEL)),
        _const_spec((1, D_MODEL)), _const_spec((1, D_MODEL)),
    ] + _route_consts_specs()
    return pl.pallas_call(
        _conv_kernel,
        out_shape=out_shapes,
        grid=(BATCH, steps),
        in_specs=in_specs,
        out_specs=out_specs,
        scratch_shapes=[
            pltpu.VMEM((t + CONV_HALO, D_MODEL), BF16),
            pltpu.VMEM((N_LANE_BLOCKS, t + CONV_HALO, LANES), F32),
            pltpu.VMEM((N_LANE_BLOCKS, t, LANES), F32),
            pltpu.VMEM((t, D_MODEL), F32),
            pltpu.VMEM((t, D_MODEL), BF16),
            pltpu.VMEM((t, D_MODEL), BF16),
            pltpu.VMEM((N_EXPERTS, LANES), F32),
        ],
        compiler_params=pltpu.CompilerParams(
            dimension_semantics=("arbitrary", "arbitrary"), vmem_limit_bytes=VMEM_LIMIT),
        name="conv_mixer",
    )(x, x, w1, b1, dw3, dwb, cg, cb, w2, b2, lg, lb, *route_consts)


def _rope_block(v, cos, sin_signed, lower_half):
    partner = jnp.where(lower_half, pltpu.roll(v, LANES - HEAD_DIM // 2, 1),
                        pltpu.roll(v, HEAD_DIM // 2, 1))
    return v * cos + partner * sin_signed


def _lower_half_mask(rows):
    lane = lax.broadcasted_iota(I32, (rows, LANES), 1)
    return (lane % HEAD_DIM) < (HEAD_DIM // 2)


def _kv_kernel(x_ref, pos_ref, invf_ref, w_ref, b_ref, k_ref, v_ref, cos_ref, sin_ref):
    t = KV_TILE
    kv = jnp.dot(x_ref[...].astype(BF16), w_ref[...], preferred_element_type=F32) + b_ref[...]
    ang = pos_ref[...].astype(F32) * invf_ref[...]
    lower = _lower_half_mask(t)
    cos = jnp.cos(ang)
    sin = jnp.sin(ang)
    sin_signed = jnp.where(lower, -sin, sin)
    cos_ref[...] = cos
    sin_ref[...] = sin_signed
    for c in range(KV_DIM // LANES):
        cols = slice(c * LANES, (c + 1) * LANES)
        k_ref[:, cols] = _rope_block(kv[:, cols], cos, sin_signed, lower).astype(BF16)
    v_ref[...] = kv[:, KV_DIM:].astype(BF16)


def _shared_kv(x2d, pos_col, invf, kv_w, kv_b):
    t = KV_TILE
    return pl.pallas_call(
        _kv_kernel,
        out_shape=(jax.ShapeDtypeStruct((N_TOK, KV_DIM), BF16),
                   jax.ShapeDtypeStruct((N_TOK, KV_DIM), BF16),
                   jax.ShapeDtypeStruct((N_TOK, LANES), F32),
                   jax.ShapeDtypeStruct((N_TOK, LANES), F32)),
        grid=(N_TOK // t,),
        in_specs=[pl.BlockSpec((t, D_MODEL), lambda i: (i, 0)),
                  pl.BlockSpec((t, 1), lambda i: (i, 0)),
                  pl.BlockSpec((1, LANES), lambda i: (0, 0)),
                  pl.BlockSpec((D_MODEL, 2 * KV_DIM), lambda i: (0, 0)),
                  pl.BlockSpec((1, 2 * KV_DIM), lambda i: (0, 0))],
        out_specs=(pl.BlockSpec((t, KV_DIM), lambda i: (i, 0)),
                   pl.BlockSpec((t, KV_DIM), lambda i: (i, 0)),
                   pl.BlockSpec((t, LANES), lambda i: (i, 0)),
                   pl.BlockSpec((t, LANES), lambda i: (i, 0))),
        compiler_params=pltpu.CompilerParams(
            dimension_semantics=("arbitrary",), vmem_limit_bytes=VMEM_LIMIT),
        name="shared_kv",
    )(x2d, pos_col, invf, kv_w, kv_b)


def _attn_kernel(x_ref, kp_ref, k_ref, vp_ref, v_ref, cos_ref, sin_ref, wq_ref, bq_ref, sink_ref,
                 wo_ref, bo_ref, lg_ref, lb_ref, rwt_ref, rb_ref, tri_ref, ltri_ref,
                 x1_ref, xsort_ref, lpos_ref, gate_ref, tstart_ref, tcnt_ref, tot_ref,
                 q_ref, kf_ref, vf_ref, o_ref, mix_ref, xb_ref, carry_ref):
    b = pl.program_id(0)
    s = pl.program_id(1)
    t = MIX_TILE

    q = jnp.dot(x_ref[0].astype(BF16), wq_ref[...], preferred_element_type=F32) + bq_ref[...]
    lower = _lower_half_mask(t)
    cos = cos_ref[...]
    sin_signed = sin_ref[...]
    scale = 1.0 / math.sqrt(HEAD_DIM)
    for c in range(N_LANE_BLOCKS):
        cols = slice(c * LANES, (c + 1) * LANES)
        q_ref[:, cols] = (_rope_block(q[:, cols], cos, sin_signed, lower) * scale).astype(BF16)

    kf_ref[0:WINDOW, :] = kp_ref[...]
    kf_ref[WINDOW:, :] = k_ref[...]
    vf_ref[0:WINDOW, :] = vp_ref[...]
    vf_ref[WINDOW:, :] = v_ref[...]

    qi = lax.broadcasted_iota(I32, (WINDOW, 2 * WINDOW), 0)
    kj = lax.broadcasted_iota(I32, (WINDOW, 2 * WINDOW), 1)
    dist = qi + WINDOW - kj
    in_window = (dist >= 0) & (dist < WINDOW)

    def block_body(n, carry):
        r0 = pl.multiple_of(n * WINDOW, WINDOW)
        first_key = jnp.where((s == 0) & (n == 0), WINDOW, 0)
        mask = in_window & (kj >= first_key)
        for kh in range(N_KV_HEADS):
            k2 = kf_ref[pl.ds(r0, 2 * WINDOW), kh * HEAD_DIM:(kh + 1) * HEAD_DIM]
            v2 = vf_ref[pl.ds(r0, 2 * WINDOW), kh * HEAD_DIM:(kh + 1) * HEAD_DIM]
            outs = []
            for g in range(GQA_GROUP):
                h = kh * GQA_GROUP + g
                qh = q_ref[pl.ds(r0, WINDOW), h * HEAD_DIM:(h + 1) * HEAD_DIM]
                sc = lax.dot_general(qh, k2, (((1,), (1,)), ((), ())),
                                     preferred_element_type=F32)
                sc = jnp.where(mask, sc, NEG_INF)
                sink = sink_ref[h]
                m = jnp.maximum(jnp.max(sc, axis=-1, keepdims=True), sink)
                p = jnp.exp(sc - m)
                denom = jnp.sum(p, axis=-1, keepdims=True) + jnp.exp(sink - m)
                pv = jnp.dot(p.astype(BF16), v2, preferred_element_type=F32)
                outs.append(pv / denom)
            o_ref[pl.ds(r0, WINDOW), kh * GQA_GROUP * HEAD_DIM:(kh + 1) * GQA_GROUP * HEAD_DIM] = (
                jnp.concatenate(outs, axis=1).astype(BF16))
        return carry
    lax.fori_loop(0, t // WINDOW, block_body, 0)

    mix_ref[...] = jnp.dot(o_ref[...], wo_ref[...], preferred_element_type=F32) + bo_ref[...]
    _deepnorm_route(x_ref, mix_ref, lg_ref, lb_ref, x1_ref, xb_ref, (b == 0) & (s == 0),
                    (rwt_ref, rb_ref, tri_ref, ltri_ref, carry_ref),
                    (xsort_ref, lpos_ref, gate_ref, tstart_ref, tcnt_ref))
    tot_ref[...] = carry_ref[...]


def _attn_layer(x, k, v, cos, sin, wq, bq, sinks, wo, bo, lg, lb, route_consts):
    t = MIX_TILE
    steps = SEQ // t
    win_per_tile = t // WINDOW
    out_shapes, out_specs = _mixer_outputs(steps)

    def tile_map(b, s):
        return (b * steps + s, 0)

    def prev_window_map(b, s):
        return (b * (SEQ // WINDOW) + jnp.maximum(s * win_per_tile - 1, 0), 0)

    in_specs = [
        pl.BlockSpec((1, t, D_MODEL), lambda b, s: (b, s, 0)),
        pl.BlockSpec((WINDOW, KV_DIM), prev_window_map),
        pl.BlockSpec((t, KV_DIM), tile_map),
        pl.BlockSpec((WINDOW, KV_DIM), prev_window_map),
        pl.BlockSpec((t, KV_DIM), tile_map),
        pl.BlockSpec((t, LANES), tile_map),
        pl.BlockSpec((t, LANES), tile_map),
        _const_spec((D_MODEL, D_MODEL)), _const_spec((1, D_MODEL)),
        pl.BlockSpec(memory_space=pltpu.SMEM),
        _const_spec((D_MODEL, D_MODEL)), _const_spec((1, D_MODEL)),
        _const_spec((1, D_MODEL)), _const_spec((1, D_MODEL)),
    ] + _route_consts_specs()
    return pl.pallas_call(
        _attn_kernel,
        out_shape=out_shapes,
        grid=(BATCH, steps),
        in_specs=in_specs,
        out_specs=out_specs,
        scratch_shapes=[
            pltpu.VMEM((t, D_MODEL), BF16),
            pltpu.VMEM((t + WINDOW, KV_DIM), BF16),
            pltpu.VMEM((t + WINDOW, KV_DIM), BF16),
            pltpu.VMEM((t, D_MODEL), BF16),
            pltpu.VMEM((t, D_MODEL), F32),
            pltpu.VMEM((t, D_MODEL), BF16),
            pltpu.VMEM((N_EXPERTS, LANES), F32),
        ],
        compiler_params=pltpu.CompilerParams(
            dimension_semantics=("arbitrary", "arbitrary"), vmem_limit_bytes=VMEM_LIMIT),
        name="attn_mixer",
    )(x, k, k, v, v, cos, sin, wq, bq, sinks, wo, bo, lg, lb, *route_consts)


def _tile_rows(tcnt_ref, tile):
    used = jnp.int32(0)
    for e in range(N_EXPERTS):
        used = used + tcnt_ref[tile * N_EXPERTS + e]
    return used


def _run_slices(tstart_ref, tcnt_ref, pstart_ref, tile, e, loff):
    n = pl.multiple_of(tcnt_ref[tile * N_EXPERTS + e], RUN_ALIGN)
    local = pl.ds(pl.multiple_of(loff, RUN_ALIGN), n)
    glob = pl.ds(pl.multiple_of(pstart_ref[e] + tstart_ref[tile * N_EXPERTS + e], RUN_ALIGN), n)
    return local, glob, n


def _zero_unused_slots(fill_ref, pend_ref, xs_ref, zero_ref, zsem):
    zero_ref[...] = jnp.zeros_like(zero_ref)

    def pad_copy(e):
        n = pl.multiple_of(pend_ref[e] - fill_ref[e], RUN_ALIGN)
        dst = xs_ref.at[pl.ds(pl.multiple_of(fill_ref[e], RUN_ALIGN), n)]
        return pltpu.make_async_copy(zero_ref.at[pl.ds(0, n)], dst, zsem), n

    def tail_copy(j):
        rows = pl.ds(pl.multiple_of(j * EXPERT_BLOCK, EXPERT_BLOCK), EXPERT_BLOCK)
        return pltpu.make_async_copy(zero_ref, xs_ref.at[rows], zsem)

    first_tail = pend_ref[N_EXPERTS - 1] // EXPERT_BLOCK
    for e in range(N_EXPERTS):
        cp, n = pad_copy(e)
        pl.when(n > 0)(cp.start)
    lax.fori_loop(first_tail, N_BLOCKS, lambda j, c: (tail_copy(j).start(), c)[1], 0)
    for e in range(N_EXPERTS):
        cp, n = pad_copy(e)
        pl.when(n > 0)(cp.wait)
    lax.fori_loop(first_tail, N_BLOCKS, lambda j, c: (tail_copy(j).wait(), c)[1], 0)


def _dispatch_kernel(tstart_ref, tcnt_ref, pstart_ref, fill_ref, pend_ref, xsort_ref, xs_ref,
                     zero_ref, sem, zsem):
    i = pl.program_id(0)

    @pl.when(i == 0)
    def _():
        _zero_unused_slots(fill_ref, pend_ref, xs_ref, zero_ref, zsem)

    total = jnp.int32(0)
    for g in range(DISPATCH_GROUP):
        tile = i * DISPATCH_GROUP + g
        loff = jnp.int32(g * SORT_SLOTS)
        for e in range(N_EXPERTS):
            local, glob, n = _run_slices(tstart_ref, tcnt_ref, pstart_ref, tile, e, loff)
            cp = pltpu.make_async_copy(xsort_ref.at[local], xs_ref.at[glob], sem)
            pl.when(n > 0)(cp.start)
            loff = loff + n
        total = total + (loff - g * SORT_SLOTS)
    total = pl.multiple_of(total, RUN_ALIGN)

    @pl.when(total > 0)
    def _():
        pltpu.make_async_copy(xsort_ref.at[pl.ds(0, total)], xs_ref.at[pl.ds(0, total)], sem).wait()


def _dispatch(tstart, tcnt, pstart, fill, pend, xsort):
    rows = DISPATCH_GROUP * SORT_SLOTS
    grid_spec = pltpu.PrefetchScalarGridSpec(
        num_scalar_prefetch=5,
        grid=(N_SORT_TILES // DISPATCH_GROUP,),
        in_specs=[pl.BlockSpec((rows, D_PACK), lambda i, *_: (i, 0))],
        out_specs=pl.BlockSpec(memory_space=pl.ANY),
        scratch_shapes=[pltpu.VMEM((EXPERT_BLOCK, D_PACK), U32),
                        pltpu.SemaphoreType.DMA, pltpu.SemaphoreType.DMA],
    )
    return pl.pallas_call(
        _dispatch_kernel,
        out_shape=jax.ShapeDtypeStruct((N_SLOTS, D_PACK), U32),
        grid_spec=grid_spec,
        compiler_params=pltpu.CompilerParams(
            dimension_semantics=("arbitrary",), vmem_limit_bytes=VMEM_LIMIT),
        name="moe_dispatch",
    )(tstart, tcnt, pstart, fill, pend, xsort)


def _expert_kernel(be_ref, nb_ref, xs_ref, wg_ref, wu_ref, wd_ref, ys_ref, wg_s, wu_s, wd_s):
    i = pl.program_id(0)
    live = i < nb_ref[0]
    new_expert = (i == 0) | (be_ref[i] != be_ref[jnp.maximum(i - 1, 0)])

    @pl.when(new_expert & live)
    def _():
        wg_s[...] = wg_ref[...].astype(BF16)
        wu_s[...] = wu_ref[...].astype(BF16)
        wd_s[...] = wd_ref[...].astype(BF16)

    @pl.when(live)
    def _():
        x = _unpack_rows(xs_ref[...])
        hg = jnp.dot(x, wg_s[...], preferred_element_type=F32)
        hu = jnp.dot(x, wu_s[...], preferred_element_type=F32)
        h = (hg * jax.nn.sigmoid(hg) * hu).astype(BF16)
        y = jnp.dot(h, wd_s[...], preferred_element_type=F32)
        ys_ref[...] = _pack_rows(y.astype(BF16).astype(F32))

    @pl.when(jnp.logical_not(live))
    def _():
        ys_ref[...] = jnp.zeros_like(ys_ref)


def _experts(layer, blk_expert, n_live, xs, w_gate, w_up, w_down):
    def x_map(i, be, nb):
        return (jnp.minimum(i, jnp.maximum(nb[0] - 1, 0)), 0)

    w_map = lambda i, be, nb: (layer, be[i], 0, 0)
    grid_spec = pltpu.PrefetchScalarGridSpec(
        num_scalar_prefetch=2,
        grid=(N_BLOCKS,),
        in_specs=[
            pl.BlockSpec((EXPERT_BLOCK, D_PACK), x_map),
            pl.BlockSpec((None, None, D_MODEL, D_EXPERT), w_map),
            pl.BlockSpec((None, None, D_MODEL, D_EXPERT), w_map),
            pl.BlockSpec((None, None, D_EXPERT, D_MODEL), w_map),
        ],
        out_specs=pl.BlockSpec((EXPERT_BLOCK, D_PACK), lambda i, be, nb: (i, 0)),
        scratch_shapes=[pltpu.VMEM((D_MODEL, D_EXPERT), BF16),
                        pltpu.VMEM((D_MODEL, D_EXPERT), BF16),
                        pltpu.VMEM((D_EXPERT, D_MODEL), BF16)],
    )
    return pl.pallas_call(
        _expert_kernel,
        out_shape=jax.ShapeDtypeStruct((N_SLOTS, D_PACK), U32),
        grid_spec=grid_spec,
        compiler_params=pltpu.CompilerParams(
            dimension_semantics=("arbitrary",), vmem_limit_bytes=VMEM_LIMIT),
        name="moe_experts",
    )(blk_expert, n_live, xs, w_gate, w_up, w_down)


def _combine_kernel(tstart_ref, tcnt_ref, pstart_ref, x_ref, ys_ref, lpos_ref, gate_ref, lg_ref,
                    lb_ref, out_ref, ybuf_ref, y_ref, sems):
    t = SORT_TILE
    i = pl.program_id(0)
    slot = i % 2

    def gather(tile, dst_slot):
        loff = jnp.int32(0)
        for e in range(N_EXPERTS):
            local, glob, n = _run_slices(tstart_ref, tcnt_ref, pstart_ref, tile, e, loff)
            cp = pltpu.make_async_copy(ys_ref.at[glob], ybuf_ref.at[dst_slot, local],
                                       sems.at[dst_slot])
            pl.when(n > 0)(cp.start)
            loff = loff + n

    @pl.when(i == 0)
    def _():
        ybuf_ref[...] = jnp.zeros_like(ybuf_ref)
        gather(0, 0)

    @pl.when(i + 1 < N_SORT_TILES)
    def _():
        gather(i + 1, 1 - slot)

    used = pl.multiple_of(_tile_rows(tcnt_ref, i), RUN_ALIGN)
    pltpu.make_async_copy(ys_ref.at[pl.ds(0, used)], ybuf_ref.at[slot, pl.ds(0, used)],
                          sems.at[slot]).wait()
    live = lax.broadcasted_iota(I32, (SORT_SLOTS, 1), 0) < used
    ys = _unpack_rows(jnp.where(live, ybuf_ref[slot], jnp.uint32(0)))
    slot = lax.broadcasted_iota(I32, (t, SORT_SLOTS), 1)
    lpos = lpos_ref[...]
    gate = gate_ref[...]
    y = None
    for k in range(TOP_K):
        unsort = jnp.where(slot == lpos[:, k:k + 1], 1.0, 0.0).astype(BF16)
        yk = gate[:, k:k + 1] * jnp.dot(unsort, ys, preferred_element_type=F32)
        y = yk if y is None else y + yk
    y_ref[...] = y

    def body(i, carry):
        rows = pl.ds(pl.multiple_of(i * ROW_CHUNK, ROW_CHUNK), ROW_CHUNK)
        z = DEEPNORM_ALPHA * x_ref[rows, :] + y_ref[rows, :]
        out_ref[rows, :] = _layer_norm(z, lg_ref[...], lb_ref[...])
        return carry
    lax.fori_loop(0, t // ROW_CHUNK, body, 0, unroll=4)


def _combine(tstart, tcnt, pstart, x2d, ys, lpos_col, gate_col, lg, lb):
    t = SORT_TILE
    grid_spec = pltpu.PrefetchScalarGridSpec(
        num_scalar_prefetch=3,
        grid=(N_SORT_TILES,),
        in_specs=[pl.BlockSpec((t, D_MODEL), lambda i, *_: (i, 0)),
                  pl.BlockSpec(memory_space=pl.ANY),
                  pl.BlockSpec((t, TOP_K), lambda i, *_: (i, 0)),
                  pl.BlockSpec((t, TOP_K), lambda i, *_: (i, 0)),
                  pl.BlockSpec((1, D_MODEL), lambda i, *_: (0, 0)),
                  pl.BlockSpec((1, D_MODEL), lambda i, *_: (0, 0))],
        out_specs=pl.BlockSpec((t, D_MODEL), lambda i, *_: (i, 0)),
        scratch_shapes=[pltpu.VMEM((2, SORT_SLOTS, D_PACK), U32),
                        pltpu.VMEM((t, D_MODEL), F32),
                        pltpu.SemaphoreType.DMA((2,))],
    )
    return pl.pallas_call(
        _combine_kernel,
        out_shape=jax.ShapeDtypeStruct((N_TOK, D_MODEL), F32),
        grid_spec=grid_spec,
        compiler_params=pltpu.CompilerParams(
            dimension_semantics=("arbitrary",), vmem_limit_bytes=VMEM_LIMIT),
        name="moe_combine",
    )(tstart, tcnt, pstart, x2d, ys, lpos_col, gate_col, lg, lb)


def _moe(layer, x1, xsort, lpos, gates, tstart, tcnt, tot, w_gate, w_up, w_down, lg, lb):
    tot = tot[:, 0].astype(I32)
    region = ((tot + EXPERT_BLOCK - 1) // EXPERT_BLOCK) * EXPERT_BLOCK
    pend = jnp.cumsum(region).astype(I32)
    pstart = pend - region
    n_live = (pend[N_EXPERTS - 1:] // EXPERT_BLOCK).astype(I32)
    blk_start = jnp.arange(N_BLOCKS, dtype=I32) * EXPERT_BLOCK
    blk_expert = jnp.minimum(jnp.sum(blk_start[:, None] >= pend[None, :], axis=1),
                             N_EXPERTS - 1).astype(I32)
    tstart_flat = tstart[:, :, 0].astype(I32).reshape(-1)
    tcnt_flat = tcnt[:, :, 0].astype(I32).reshape(-1)

    xs = _dispatch(tstart_flat, tcnt_flat, pstart, pstart + tot, pend, xsort)
    ys = _experts(layer, blk_expert, n_live, xs, w_gate, w_up, w_down)
    out = _combine(tstart_flat, tcnt_flat, pstart, x1.reshape(N_TOK, D_MODEL), ys, lpos.T, gates.T,
                   lg, lb)
    return out.reshape(BATCH, SEQ, D_MODEL)


def kernel(x, positions, conv_w1, conv_b1, conv_dw, conv_dwb, conv_ln_g, conv_ln_b, conv_w2,
           conv_b2, kv_w, kv_b, attn_wq, attn_bq, attn_sinks, attn_wo, attn_bo, router_w,
           router_b, moe_w_gate, moe_w_up, moe_w_down, ln_g, ln_b):
    row = lambda v: v.reshape(1, -1)
    rwt = router_w.T.astype(BF16)
    rb = router_b.astype(F32).reshape(N_EXPERTS, 1)
    idx = jnp.arange(SORT_TILE, dtype=I32)
    tri = (idx[:, None] < idx[None, :]).astype(BF16)
    eidx = jnp.arange(N_EXPERTS, dtype=I32)
    ltri = (eidx[None, :] < eidx[:, None]).astype(BF16)
    route_consts = (rwt, rb, tri, ltri)
    inv_freq = ROPE_THETA ** (-jnp.arange(0, HEAD_DIM, 2, dtype=F32) / HEAD_DIM)
    invf = jnp.tile(inv_freq, LANES // (HEAD_DIM // 2)).reshape(1, LANES)
    pos_col = positions.reshape(N_TOK, 1)

    k = v = cos = sin = None
    for layer in range(DEPTH):
        lg0, lb0 = row(ln_g[layer, 0]), row(ln_b[layer, 0])
        if layer < N_CONV_LAYERS:
            i = layer
            dw3 = conv_dw[i].reshape(CONV_WIDTH, N_LANE_BLOCKS, LANES).transpose(1, 0, 2)
            outs = _conv_layer(
                x, conv_w1[i].astype(BF16), row(conv_b1[i]), dw3, row(conv_dwb[i]),
                row(conv_ln_g[i]), row(conv_ln_b[i]), conv_w2[i].astype(BF16), row(conv_b2[i]),
                lg0, lb0, route_consts)
        else:
            if layer == N_CONV_LAYERS:
                k, v, cos, sin = _shared_kv(x.reshape(N_TOK, D_MODEL), pos_col, invf,
                                            kv_w.astype(BF16), row(kv_b))
            j = layer - N_CONV_LAYERS
            outs = _attn_layer(
                x, k, v, cos, sin, attn_wq[j].astype(BF16), row(attn_bq[j]),
                attn_sinks[j].astype(F32), attn_wo[j].astype(BF16), row(attn_bo[j]),
                lg0, lb0, route_consts)
        x = _moe(layer, *outs, moe_w_gate, moe_w_up, moe_w_down,
                 row(ln_g[layer, 1]), row(ln_b[layer, 1]))
    return x
```

```python
import math

import jax
import jax.numpy as jnp
from jax import lax
from jax.experimental import pallas as pl
from jax.experimental.pallas import tpu as pltpu

D_MODEL = 1024
BATCH = 8
SEQ = 4096
DEPTH = 4
N_TOK = BATCH * SEQ
N_CONV_LAYERS = DEPTH // 2
CONV_WIDTH = 31
HEAD_DIM = 64
N_Q_HEADS = D_MODEL // HEAD_DIM
N_KV_HEADS = 4
GQA_GROUP = N_Q_HEADS // N_KV_HEADS
KV_DIM = N_KV_HEADS * HEAD_DIM
WINDOW = 128
ROPE_THETA = 10000.0
N_EXPERTS = 16
N_GROUPS = 4
EXPERTS_PER_GROUP = N_EXPERTS // N_GROUPS
TOP_K = 2
D_EXPERT = D_MODEL // 2
LN_EPS = 1e-5
NEG_INF = -1e30
DEEPNORM_ALPHA = (2.0 * DEPTH) ** 0.25

LANES = 128
SUBLANES = 8
MXU_DIM = 256
N_LANE_BLOCKS = D_MODEL // LANES
D_PACK = D_MODEL // 2

MIX_TILE = 512
CONV_HALO = 32
ROW_CHUNK = 64
KV_TILE = 1024

SORT_TILE = 256
RUN_ALIGN = SUBLANES
SORT_SLOTS = TOP_K * SORT_TILE + LANES
N_SORT_TILES = N_TOK // SORT_TILE
SORT_PER_MIX = MIX_TILE // SORT_TILE
DISPATCH_GROUP = 4
EXPERT_BLOCK = 512
_MAX_RUN_ROWS = N_TOK * TOP_K + N_SORT_TILES * N_EXPERTS * (RUN_ALIGN - 1)
N_BLOCKS = (_MAX_RUN_ROWS + N_EXPERTS * (EXPERT_BLOCK - 1)) // EXPERT_BLOCK + 1
N_SLOTS = N_BLOCKS * EXPERT_BLOCK

VMEM_LIMIT = 56 * 1024 * 1024

F32 = jnp.float32
BF16 = jnp.bfloat16
U32 = jnp.uint32
I32 = jnp.int32

assert SORT_SLOTS >= TOP_K * SORT_TILE + N_EXPERTS * (RUN_ALIGN - 1) and SORT_SLOTS % RUN_ALIGN == 0


def _layer_norm(z, g, b):
    mu = jnp.mean(z, axis=-1, keepdims=True)
    zc = z - mu
    var = jnp.mean(zc * zc, axis=-1, keepdims=True)
    return zc * lax.rsqrt(var + LN_EPS) * g + b


def _pack_rows(v):
    hi = lax.bitcast_convert_type(v[:, :D_PACK], U32) & jnp.uint32(0xFFFF0000)
    lo = lax.bitcast_convert_type(v[:, D_PACK:], U32) >> 16
    return hi | lo


def _unpack_rows(w):
    hi = lax.bitcast_convert_type(w & jnp.uint32(0xFFFF0000), F32)
    lo = lax.bitcast_convert_type(w << 16, F32)
    return jnp.concatenate([hi, lo], axis=1).astype(BF16)


def _top2_sum(a, b, c, d):
    hi1, lo1 = jnp.maximum(a, b), jnp.minimum(a, b)
    hi2, lo2 = jnp.maximum(c, d), jnp.minimum(c, d)
    top1 = jnp.maximum(hi1, hi2)
    top2 = jnp.maximum(jnp.minimum(hi1, hi2), jnp.maximum(lo1, lo2))
    return top1 + top2


def _argmax4(vals):
    best, idx = vals[0], jnp.zeros(vals[0].shape, I32)
    for j in range(1, 4):
        better = vals[j] > best
        idx = jnp.where(better, j, idx)
        best = jnp.where(better, vals[j], best)
    return idx


def _pick4(idx, vals):
    out = vals[3]
    for j in (2, 1, 0):
        out = jnp.where(idx == j, vals[j], out)
    return out


def _route_sort(half, xb, rwt_ref, rb_ref, tri_ref, ltri_ref, carry_ref,
                xsort_ref, lpos_ref, gate_ref, tstart_ref, tcnt_ref):
    t = SORT_TILE
    cols = slice(half * t, (half + 1) * t)
    logits = lax.dot_general(rwt_ref[...], xb, (((1,), (1,)), ((), ())),
                             preferred_element_type=F32)
    aff = jax.nn.sigmoid(logits)
    sel = aff + rb_ref[...]
    sel_rows = [sel[e:e + 1, :] for e in range(N_EXPERTS)]
    aff_rows = [aff[e:e + 1, :] for e in range(N_EXPERTS)]
    gscore = [_top2_sum(*sel_rows[4 * g:4 * g + 4]) for g in range(N_GROUPS)]
    grp = _argmax4(gscore)
    sel_in = [_pick4(grp, [sel_rows[4 * g + j] for g in range(N_GROUPS)]) for j in range(4)]
    aff_in = [_pick4(grp, [aff_rows[4 * g + j] for g in range(N_GROUPS)]) for j in range(4)]
    i0 = _argmax4(sel_in)
    i1 = _argmax4([jnp.where(i0 == j, -jnp.inf, sel_in[j]) for j in range(4)])
    a0 = _pick4(i0, aff_in)
    a1 = _pick4(i1, aff_in)
    gsum = a0 + a1
    gate_ref[0:1, cols] = a0 / gsum
    gate_ref[1:2, cols] = a1 / gsum
    e0 = grp * EXPERTS_PER_GROUP + i0
    e1 = grp * EXPERTS_PER_GROUP + i1

    eiota = lax.broadcasted_iota(I32, (N_EXPERTS, t), 0)
    hit0 = eiota == e0
    hit1 = eiota == e1
    onehot = jnp.where(hit0 | hit1, 1.0, 0.0)
    before = jnp.dot(onehot.astype(BF16), tri_ref[...], preferred_element_type=F32)
    cnt = jnp.sum(onehot, axis=1, keepdims=True)
    run = jnp.floor((cnt + (RUN_ALIGN - 1)) * (1.0 / RUN_ALIGN)) * RUN_ALIGN
    run_b = jnp.broadcast_to(run, (N_EXPERTS, LANES))
    loff = jnp.dot(ltri_ref[...], run_b.astype(BF16), preferred_element_type=F32)
    pos = before + loff[:, 0:1]
    lp0 = jnp.sum(jnp.where(hit0, pos, 0.0), axis=0, keepdims=True).astype(I32)
    lp1 = jnp.sum(jnp.where(hit1, pos, 0.0), axis=0, keepdims=True).astype(I32)
    lpos_ref[0:1, cols] = lp0
    lpos_ref[1:2, cols] = lp1

    slot = lax.broadcasted_iota(I32, (SORT_SLOTS, t), 0)
    perm = jnp.where((slot == lp0) | (slot == lp1), 1.0, 0.0).astype(BF16)
    rows = jnp.dot(perm, xb, preferred_element_type=F32)
    xsort_ref[half * SORT_SLOTS:(half + 1) * SORT_SLOTS, :] = _pack_rows(rows)

    tstart_ref[half] = carry_ref[...]
    tcnt_ref[half] = run_b
    carry_ref[...] = carry_ref[...] + run_b


def _deepnorm_route(x_ref, mix_ref, lg_ref, lb_ref, x1_ref, xb_ref, first, route_refs, out_refs):
    carry_ref = route_refs[-1]

    @pl.when(first)
    def _():
        carry_ref[...] = jnp.zeros_like(carry_ref)

    def body(i, c):
        rows = pl.ds(pl.multiple_of(i * ROW_CHUNK, ROW_CHUNK), ROW_CHUNK)
        z = DEEPNORM_ALPHA * x_ref[0, rows, :] + mix_ref[rows, :]
        y = _layer_norm(z, lg_ref[...], lb_ref[...])
        x1_ref[0, rows, :] = y
        xb_ref[rows, :] = y.astype(BF16)
        return c
    lax.fori_loop(0, MIX_TILE // ROW_CHUNK, body, 0, unroll=4)

    for half in range(SORT_PER_MIX):
        xb = xb_ref[half * SORT_TILE:(half + 1) * SORT_TILE, :]
        _route_sort(half, xb, *route_refs, *out_refs)


def _conv_kernel(x_ref, xh_ref, w1_ref, b1_ref, dw_ref, dwb_ref, cg_ref, cb_ref, w2_ref, b2_ref,
                 lg_ref, lb_ref, rwt_ref, rb_ref, tri_ref, ltri_ref,
                 x1_ref, xsort_ref, lpos_ref, gate_ref, tstart_ref, tcnt_ref, tot_ref,
                 lhs_ref, h_ref, c_ref, mix_ref, act_ref, xb_ref, carry_ref):
    b = pl.program_id(0)
    s = pl.program_id(1)
    t = MIX_TILE

    lhs_ref[0:CONV_HALO, :] = xh_ref[0].astype(BF16)
    lhs_ref[CONV_HALO:, :] = x_ref[0].astype(BF16)
    lhs = lhs_ref[...]
    for cp in range(D_MODEL // MXU_DIM):
        ca = slice(cp * MXU_DIM, (cp + 1) * MXU_DIM)
        cg = slice(D_MODEL + cp * MXU_DIM, D_MODEL + (cp + 1) * MXU_DIM)
        a = jnp.dot(lhs, w1_ref[:, ca], preferred_element_type=F32) + b1_ref[:, ca]
        gt = jnp.dot(lhs, w1_ref[:, cg], preferred_element_type=F32) + b1_ref[:, cg]
        h = a * jax.nn.sigmoid(gt)
        for j in range(MXU_DIM // LANES):
            h_ref[cp * (MXU_DIM // LANES) + j] = h[:, j * LANES:(j + 1) * LANES]

    @pl.when(s == 0)
    def _():
        h_ref[:, 0:CONV_HALO, :] = jnp.zeros((N_LANE_BLOCKS, CONV_HALO, LANES), F32)

    off0 = CONV_HALO - (CONV_WIDTH - 1)

    def conv_body(c, carry):
        for i in range(t // ROW_CHUNK):
            acc = jnp.zeros((ROW_CHUNK, LANES), F32)
            for k in range(CONV_WIDTH):
                r0 = i * ROW_CHUNK + off0 + k
                acc = acc + dw_ref[c, k:k + 1, :] * h_ref[c, r0:r0 + ROW_CHUNK, :]
            c_ref[c, i * ROW_CHUNK:(i + 1) * ROW_CHUNK, :] = acc
        return carry
    lax.fori_loop(0, N_LANE_BLOCKS, conv_body, 0)

    def act_body(i, carry):
        rows = pl.ds(pl.multiple_of(i * ROW_CHUNK, ROW_CHUNK), ROW_CHUNK)
        z = jnp.concatenate([c_ref[c, rows, :] for c in range(N_LANE_BLOCKS)], axis=1)
        y = _layer_norm(z + dwb_ref[...], cg_ref[...], cb_ref[...])
        act_ref[rows, :] = (y * jax.nn.sigmoid(y)).astype(BF16)
        return carry
    lax.fori_loop(0, t // ROW_CHUNK, act_body, 0, unroll=4)

    mix_ref[...] = jnp.dot(act_ref[...], w2_ref[...], preferred_element_type=F32) + b2_ref[...]
    _deepnorm_route(x_ref, mix_ref, lg_ref, lb_ref, x1_ref, xb_ref, (b == 0) & (s == 0),
                    (rwt_ref, rb_ref, tri_ref, ltri_ref, carry_ref),
                    (xsort_ref, lpos_ref, gate_ref, tstart_ref, tcnt_ref))
    tot_ref[...] = carry_ref[...]


def _const_spec(shape):
    return pl.BlockSpec(shape, lambda b, s: (0,) * len(shape))


def _route_consts_specs():
    return [_const_spec((N_EXPERTS, D_MODEL)), _const_spec((N_EXPERTS, 1)),
            _const_spec((SORT_TILE, SORT_TILE)), _const_spec((N_EXPERTS, N_EXPERTS))]


def _mixer_outputs(steps):
    t = MIX_TILE
    step = lambda b, s: b * steps + s
    shapes = (jax.ShapeDtypeStruct((BATCH, SEQ, D_MODEL), F32),
              jax.ShapeDtypeStruct((N_SORT_TILES * SORT_SLOTS, D_PACK), U32),
              jax.ShapeDtypeStruct((TOP_K, N_TOK), I32),
              jax.ShapeDtypeStruct((TOP_K, N_TOK), F32),
              jax.ShapeDtypeStruct((N_SORT_TILES, N_EXPERTS, LANES), F32),
              jax.ShapeDtypeStruct((N_SORT_TILES, N_EXPERTS, LANES), F32),
              jax.ShapeDtypeStruct((N_EXPERTS, LANES), F32))
    specs = (pl.BlockSpec((1, t, D_MODEL), lambda b, s: (b, s, 0)),
             pl.BlockSpec((SORT_PER_MIX * SORT_SLOTS, D_PACK), lambda b, s: (step(b, s), 0)),
             pl.BlockSpec((TOP_K, t), lambda b, s: (0, step(b, s))),
             pl.BlockSpec((TOP_K, t), lambda b, s: (0, step(b, s))),
             pl.BlockSpec((SORT_PER_MIX, N_EXPERTS, LANES), lambda b, s: (step(b, s), 0, 0)),
             pl.BlockSpec((SORT_PER_MIX, N_EXPERTS, LANES), lambda b, s: (step(b, s), 0, 0)),
             pl.BlockSpec((N_EXPERTS, LANES), lambda b, s: (0, 0)))
    return shapes, specs


def _conv_layer(x, w1, b1, dw3, dwb, cg, cb, w2, b2, lg, lb, route_consts):
    t = MIX_TILE
    steps = SEQ // t
    halo_per_tile = t // CONV_HALO
    out_shapes, out_specs = _mixer_outputs(steps)
    in_specs = [
        pl.BlockSpec((1, t, D_MODEL), lambda b, s: (b, s, 0)),
        pl.BlockSpec((1, CONV_HALO, D_MODEL),
                     lambda b, s: (b, jnp.maximum(s * halo_per_tile - 1, 0), 0)),
        _const_spec((D_MODEL, 2 * D_MODEL)), _const_spec((1, 2 * D_MODEL)),
        _const_spec((N_LANE_BLOCKS, CONV_WIDTH, LANES)), _const_spec((1, D_MODEL)),
        _const_spec((1, D_MODEL)), _const_spec((1, D_MODEL)),
        _const_spec((D_MODEL, D_MODEL)), _const_spec((1, D_MODEL)),
        _const_spec((1, D_MODEL)), _const_spec((1, D_MODEL)),
    ] + _route_consts_specs()
    return pl.pallas_call(
        _conv_kernel,
        out_shape=out_shapes,
        grid=(BATCH, steps),
        in_specs=in_specs,
        out_specs=out_specs,
        scratch_shapes=[
            pltpu.VMEM((t + CONV_HALO, D_MODEL), BF16),
            pltpu.VMEM((N_LANE_BLOCKS, t + CONV_HALO, LANES), F32),
            pltpu.VMEM((N_LANE_BLOCKS, t, LANES), F32),
            pltpu.VMEM((t, D_MODEL), F32),
            pltpu.VMEM((t, D_MODEL), BF16),
            pltpu.VMEM((t, D_MODEL), BF16),
            pltpu.VMEM((N_EXPERTS, LANES), F32),
        ],
        compiler_params=pltpu.CompilerParams(
            dimension_semantics=("arbitrary", "arbitrary"), vmem_limit_bytes=VMEM_LIMIT),
        name="conv_mixer",
    )(x, x, w1, b1, dw3, dwb, cg, cb, w2, b2, lg, lb, *route_consts)


def _rope_block(v, cos, sin_signed, lower_half):
    partner = jnp.where(lower_half, pltpu.roll(v, LANES - HEAD_DIM // 2, 1),
                        pltpu.roll(v, HEAD_DIM // 2, 1))
    return v * cos + partner * sin_signed


def _lower_half_mask(rows):
    lane = lax.broadcasted_iota(I32, (rows, LANES), 1)
    return (lane % HEAD_DIM) < (HEAD_DIM // 2)


def _kv_kernel(x_ref, pos_ref, invf_ref, w_ref, b_ref, k_ref, v_ref, cos_ref, sin_ref):
    t = KV_TILE
    kv = jnp.dot(x_ref[...].astype(BF16), w_ref[...], preferred_element_type=F32) + b_ref[...]
    ang = pos_ref[...].astype(F32) * invf_ref[...]
    lower = _lower_half_mask(t)
    cos = jnp.cos(ang)
    sin = jnp.sin(ang)
    sin_signed = jnp.where(lower, -sin, sin)
    cos_ref[...] = cos
    sin_ref[...] = sin_signed
    for c in range(KV_DIM // LANES):
        cols = slice(c * LANES, (c + 1) * LANES)
        k_ref[:, cols] = _rope_block(kv[:, cols], cos, sin_signed, lower).astype(BF16)
    v_ref[...] = kv[:, KV_DIM:].astype(BF16)


def _shared_kv(x2d, pos_col, invf, kv_w, kv_b):
    t = KV_TILE
    return pl.pallas_call(
        _kv_kernel,
        out_shape=(jax.ShapeDtypeStruct((N_TOK, KV_DIM), BF16),
                   jax.ShapeDtypeStruct((N_TOK, KV_DIM), BF16),
                   jax.ShapeDtypeStruct((N_TOK, LANES), F32),
                   jax.ShapeDtypeStruct((N_TOK, LANES), F32)),
        grid=(N_TOK // t,),
        in_specs=[pl.BlockSpec((t, D_MODEL), lambda i: (i, 0)),
                  pl.BlockSpec((t, 1), lambda i: (i, 0)),
                  pl.BlockSpec((1, LANES), lambda i: (0, 0)),
                  pl.BlockSpec((D_MODEL, 2 * KV_DIM), lambda i: (0, 0)),
                  pl.BlockSpec((1, 2 * KV_DIM), lambda i: (0, 0))],
        out_specs=(pl.BlockSpec((t, KV_DIM), lambda i: (i, 0)),
                   pl.BlockSpec((t, KV_DIM), lambda i: (i, 0)),
                   pl.BlockSpec((t, LANES), lambda i: (i, 0)),
                   pl.BlockSpec((t, LANES), lambda i: (i, 0))),
        compiler_params=pltpu.CompilerParams(
            dimension_semantics=("arbitrary",), vmem_limit_bytes=VMEM_LIMIT),
        name="shared_kv",
    )(x2d, pos_col, invf, kv_w, kv_b)


def _attn_kernel(x_ref, kp_ref, k_ref, vp_ref, v_ref, cos_ref, sin_ref, wq_ref, bq_ref, sink_ref,
                 wo_ref, bo_ref, lg_ref, lb_ref, rwt_ref, rb_ref, tri_ref, ltri_ref,
                 x1_ref, xsort_ref, lpos_ref, gate_ref, tstart_ref, tcnt_ref, tot_ref,
                 q_ref, kf_ref, vf_ref, o_ref, mix_ref, xb_ref, carry_ref):
    b = pl.program_id(0)
    s = pl.program_id(1)
    t = MIX_TILE

    q = jnp.dot(x_ref[0].astype(BF16), wq_ref[...], preferred_element_type=F32) + bq_ref[...]
    lower = _lower_half_mask(t)
    cos = cos_ref[...]
    sin_signed = sin_ref[...]
    scale = 1.0 / math.sqrt(HEAD_DIM)
    for c in range(N_LANE_BLOCKS):
        cols = slice(c * LANES, (c + 1) * LANES)
        q_ref[:, cols] = (_rope_block(q[:, cols], cos, sin_signed, lower) * scale).astype(BF16)

    kf_ref[0:WINDOW, :] = kp_ref[...]
    kf_ref[WINDOW:, :] = k_ref[...]
    vf_ref[0:WINDOW, :] = vp_ref[...]
    vf_ref[WINDOW:, :] = v_ref[...]

    stacked = (GQA_GROUP * WINDOW, 2 * WINDOW)
    qi = lax.broadcasted_iota(I32, stacked, 0) % WINDOW
    kj = lax.broadcasted_iota(I32, stacked, 1)
    row_group = lax.broadcasted_iota(I32, (GQA_GROUP * WINDOW, 1), 0) // WINDOW
    dist = qi + WINDOW - kj
    in_window = (dist >= 0) & (dist < WINDOW)

    def block_body(n, carry):
        r0 = pl.multiple_of(n * WINDOW, WINDOW)
        first_key = jnp.where((s == 0) & (n == 0), WINDOW, 0)
        mask = in_window & (kj >= first_key)
        for kh in range(N_KV_HEADS):
            k2 = kf_ref[pl.ds(r0, 2 * WINDOW), kh * HEAD_DIM:(kh + 1) * HEAD_DIM]
            v2 = vf_ref[pl.ds(r0, 2 * WINDOW), kh * HEAD_DIM:(kh + 1) * HEAD_DIM]
            heads = [kh * GQA_GROUP + g for g in range(GQA_GROUP)]
            qs = jnp.concatenate(
                [q_ref[pl.ds(r0, WINDOW), h * HEAD_DIM:(h + 1) * HEAD_DIM] for h in heads], axis=0)
            sc = lax.dot_general(qs, k2, (((1,), (1,)), ((), ())),
                                 preferred_element_type=F32)
            sc = jnp.where(mask, sc, NEG_INF)
            sink = jnp.full((GQA_GROUP * WINDOW, 1), sink_ref[heads[-1]], F32)
            for g in range(GQA_GROUP - 1):
                sink = jnp.where(row_group == g, sink_ref[heads[g]], sink)
            m = jnp.maximum(jnp.max(sc, axis=-1, keepdims=True), sink)
            p = jnp.exp(sc - m)
            denom = jnp.sum(p, axis=-1, keepdims=True) + jnp.exp(sink - m)
            pv = jnp.dot(p.astype(BF16), v2, preferred_element_type=F32) / denom
            outs = [pv[g * WINDOW:(g + 1) * WINDOW, :] for g in range(GQA_GROUP)]
            o_ref[pl.ds(r0, WINDOW), kh * GQA_GROUP * HEAD_DIM:(kh + 1) * GQA_GROUP * HEAD_DIM] = (
                jnp.concatenate(outs, axis=1).astype(BF16))
        return carry
    lax.fori_loop(0, t // WINDOW, block_body, 0)

    mix_ref[...] = jnp.dot(o_ref[...], wo_ref[...], preferred_element_type=F32) + bo_ref[...]
    _deepnorm_route(x_ref, mix_ref, lg_ref, lb_ref, x1_ref, xb_ref, (b == 0) & (s == 0),
                    (rwt_ref, rb_ref, tri_ref, ltri_ref, carry_ref),
                    (xsort_ref, lpos_ref, gate_ref, tstart_ref, tcnt_ref))
    tot_ref[...] = carry_ref[...]


def _attn_layer(x, k, v, cos, sin, wq, bq, sinks, wo, bo, lg, lb, route_consts):
    t = MIX_TILE
    steps = SEQ // t
    win_per_tile = t // WINDOW
    out_shapes, out_specs = _mixer_outputs(steps)

    def tile_map(b, s):
        return (b * steps + s, 0)

    def prev_window_map(b, s):
        return (b * (SEQ // WINDOW) + jnp.maximum(s * win_per_tile - 1, 0), 0)

    in_specs = [
        pl.BlockSpec((1, t, D_MODEL), lambda b, s: (b, s, 0)),
        pl.BlockSpec((WINDOW, KV_DIM), prev_window_map),
        pl.BlockSpec((t, KV_DIM), tile_map),
        pl.BlockSpec((WINDOW, KV_DIM), prev_window_map),
        pl.BlockSpec((t, KV_DIM), tile_map),
        pl.BlockSpec((t, LANES), tile_map),
        pl.BlockSpec((t, LANES), tile_map),
        _const_spec((D_MODEL, D_MODEL)), _const_spec((1, D_MODEL)),
        pl.BlockSpec(memory_space=pltpu.SMEM),
        _const_spec((D_MODEL, D_MODEL)), _const_spec((1, D_MODEL)),
        _const_spec((1, D_MODEL)), _const_spec((1, D_MODEL)),
    ] + _route_consts_specs()
    return pl.pallas_call(
        _attn_kernel,
        out_shape=out_shapes,
        grid=(BATCH, steps),
        in_specs=in_specs,
        out_specs=out_specs,
        scratch_shapes=[
            pltpu.VMEM((t, D_MODEL), BF16),
            pltpu.VMEM((t + WINDOW, KV_DIM), BF16),
            pltpu.VMEM((t + WINDOW, KV_DIM), BF16),
            pltpu.VMEM((t, D_MODEL), BF16),
            pltpu.VMEM((t, D_MODEL), F32),
            pltpu.VMEM((t, D_MODEL), BF16),
            pltpu.VMEM((N_EXPERTS, LANES), F32),
        ],
        compiler_params=pltpu.CompilerParams(
            dimension_semantics=("arbitrary", "arbitrary"), vmem_limit_bytes=VMEM_LIMIT),
        name="attn_mixer",
    )(x, k, k, v, v, cos, sin, wq, bq, sinks, wo, bo, lg, lb, *route_consts)


def _tile_rows(tcnt_ref, tile):
    used = jnp.int32(0)
    for e in range(N_EXPERTS):
        used = used + tcnt_ref[tile * N_EXPERTS + e]
    return used


def _run_slices(tstart_ref, tcnt_ref, pstart_ref, tile, e, loff):
    n = pl.multiple_of(tcnt_ref[tile * N_EXPERTS + e], RUN_ALIGN)
    local = pl.ds(pl.multiple_of(loff, RUN_ALIGN), n)
    glob = pl.ds(pl.multiple_of(pstart_ref[e] + tstart_ref[tile * N_EXPERTS + e], RUN_ALIGN), n)
    return local, glob, n


def _zero_unused_slots(fill_ref, pend_ref, xs_ref, zero_ref, zsem):
    zero_ref[...] = jnp.zeros_like(zero_ref)

    def pad_copy(e):
        n = pl.multiple_of(pend_ref[e] - fill_ref[e], RUN_ALIGN)
        dst = xs_ref.at[pl.ds(pl.multiple_of(fill_ref[e], RUN_ALIGN), n)]
        return pltpu.make_async_copy(zero_ref.at[pl.ds(0, n)], dst, zsem), n

    def tail_copy(j):
        rows = pl.ds(pl.multiple_of(j * EXPERT_BLOCK, EXPERT_BLOCK), EXPERT_BLOCK)
        return pltpu.make_async_copy(zero_ref, xs_ref.at[rows], zsem)

    first_tail = pend_ref[N_EXPERTS - 1] // EXPERT_BLOCK
    for e in range(N_EXPERTS):
        cp, n = pad_copy(e)
        pl.when(n > 0)(cp.start)
    lax.fori_loop(first_tail, N_BLOCKS, lambda j, c: (tail_copy(j).start(), c)[1], 0)
    for e in range(N_EXPERTS):
        cp, n = pad_copy(e)
        pl.when(n > 0)(cp.wait)
    lax.fori_loop(first_tail, N_BLOCKS, lambda j, c: (tail_copy(j).wait(), c)[1], 0)


def _dispatch_kernel(tstart_ref, tcnt_ref, pstart_ref, fill_ref, pend_ref, xsort_ref, xs_ref,
                     zero_ref, sem, zsem):
    i = pl.program_id(0)

    @pl.when(i == 0)
    def _():
        _zero_unused_slots(fill_ref, pend_ref, xs_ref, zero_ref, zsem)

    total = jnp.int32(0)
    for g in range(DISPATCH_GROUP):
        tile = i * DISPATCH_GROUP + g
        loff = jnp.int32(g * SORT_SLOTS)
        for e in range(N_EXPERTS):
            local, glob, n = _run_slices(tstart_ref, tcnt_ref, pstart_ref, tile, e, loff)
            cp = pltpu.make_async_copy(xsort_ref.at[local], xs_ref.at[glob], sem)
            pl.when(n > 0)(cp.start)
            loff = loff + n
        total = total + (loff - g * SORT_SLOTS)
    total = pl.multiple_of(total, RUN_ALIGN)

    @pl.when(total > 0)
    def _():
        pltpu.make_async_copy(xsort_ref.at[pl.ds(0, total)], xs_ref.at[pl.ds(0, total)], sem).wait()


def _dispatch(tstart, tcnt, pstart, fill, pend, xsort):
    rows = DISPATCH_GROUP * SORT_SLOTS
    grid_spec = pltpu.PrefetchScalarGridSpec(
        num_scalar_prefetch=5,
        grid=(N_SORT_TILES // DISPATCH_GROUP,),
        in_specs=[pl.BlockSpec((rows, D_PACK), lambda i, *_: (i, 0))],
        out_specs=pl.BlockSpec(memory_space=pl.ANY),
        scratch_shapes=[pltpu.VMEM((EXPERT_BLOCK, D_PACK), U32),
                        pltpu.SemaphoreType.DMA, pltpu.SemaphoreType.DMA],
    )
    return pl.pallas_call(
        _dispatch_kernel,
        out_shape=jax.ShapeDtypeStruct((N_SLOTS, D_PACK), U32),
        grid_spec=grid_spec,
        compiler_params=pltpu.CompilerParams(
            dimension_semantics=("arbitrary",), vmem_limit_bytes=VMEM_LIMIT),
        name="moe_dispatch",
    )(tstart, tcnt, pstart, fill, pend, xsort)


def _expert_kernel(be_ref, nb_ref, xs_ref, wg_ref, wu_ref, wd_ref, ys_ref, wg_s, wu_s, wd_s):
    i = pl.program_id(0)
    live = i < nb_ref[0]
    new_expert = (i == 0) | (be_ref[i] != be_ref[jnp.maximum(i - 1, 0)])

    @pl.when(new_expert & live)
    def _():
        wg_s[...] = wg_ref[...].astype(BF16)
        wu_s[...] = wu_ref[...].astype(BF16)
        wd_s[...] = wd_ref[...].astype(BF16)

    @pl.when(live)
    def _():
        x = _unpack_rows(xs_ref[...])
        hg = jnp.dot(x, wg_s[...], preferred_element_type=F32)
        hu = jnp.dot(x, wu_s[...], preferred_element_type=F32)
        h = (hg * jax.nn.sigmoid(hg) * hu).astype(BF16)
        y = jnp.dot(h, wd_s[...], preferred_element_type=F32)
        ys_ref[...] = _pack_rows(y.astype(BF16).astype(F32))

    @pl.when(jnp.logical_not(live))
    def _():
        ys_ref[...] = jnp.zeros_like(ys_ref)


def _experts(layer, blk_expert, n_live, xs, w_gate, w_up, w_down):
    def x_map(i, be, nb):
        return (jnp.minimum(i, jnp.maximum(nb[0] - 1, 0)), 0)

    w_map = lambda i, be, nb: (layer, be[i], 0, 0)
    grid_spec = pltpu.PrefetchScalarGridSpec(
        num_scalar_prefetch=2,
        grid=(N_BLOCKS,),
        in_specs=[
            pl.BlockSpec((EXPERT_BLOCK, D_PACK), x_map),
            pl.BlockSpec((None, None, D_MODEL, D_EXPERT), w_map),
            pl.BlockSpec((None, None, D_MODEL, D_EXPERT), w_map),
            pl.BlockSpec((None, None, D_EXPERT, D_MODEL), w_map),
        ],
        out_specs=pl.BlockSpec((EXPERT_BLOCK, D_PACK), lambda i, be, nb: (i, 0)),
        scratch_shapes=[pltpu.VMEM((D_MODEL, D_EXPERT), BF16),
                        pltpu.VMEM((D_MODEL, D_EXPERT), BF16),
                        pltpu.VMEM((D_EXPERT, D_MODEL), BF16)],
    )
    return pl.pallas_call(
        _expert_kernel,
        out_shape=jax.ShapeDtypeStruct((N_SLOTS, D_PACK), U32),
        grid_spec=grid_spec,
        compiler_params=pltpu.CompilerParams(
            dimension_semantics=("arbitrary",), vmem_limit_bytes=VMEM_LIMIT),
        name="moe_experts",
    )(blk_expert, n_live, xs, w_gate, w_up, w_down)


def _combine_kernel(tstart_ref, tcnt_ref, pstart_ref, x_ref, ys_ref, lpos_ref, gate_ref, lg_ref,
                    lb_ref, out_ref, ybuf_ref, y_ref, sems):
    t = SORT_TILE
    i = pl.program_id(0)
    slot = i % 2

    def gather(tile, dst_slot):
        loff = jnp.int32(0)
        for e in range(N_EXPERTS):
            local, glob, n = _run_slices(tstart_ref, tcnt_ref, pstart_ref, tile, e, loff)
            cp = pltpu.make_async_copy(ys_ref.at[glob], ybuf_ref.at[dst_slot, local],
                                       sems.at[dst_slot])
            pl.when(n > 0)(cp.start)
            loff = loff + n

    @pl.when(i == 0)
    def _():
        ybuf_ref[...] = jnp.zeros_like(ybuf_ref)
        gather(0, 0)

    @pl.when(i + 1 < N_SORT_TILES)
    def _():
        gather(i + 1, 1 - slot)

    used = pl.multiple_of(_tile_rows(tcnt_ref, i), RUN_ALIGN)
    pltpu.make_async_copy(ys_ref.at[pl.ds(0, used)], ybuf_ref.at[slot, pl.ds(0, used)],
                          sems.at[slot]).wait()
    live = lax.broadcasted_iota(I32, (SORT_SLOTS, 1), 0) < used
    ys = _unpack_rows(jnp.where(live, ybuf_ref[slot], jnp.uint32(0)))
    slot = lax.broadcasted_iota(I32, (t, SORT_SLOTS), 1)
    lpos = lpos_ref[...]
    gate = gate_ref[...]
    y = None
    for k in range(TOP_K):
        unsort = jnp.where(slot == lpos[:, k:k + 1], 1.0, 0.0).astype(BF16)
        yk = gate[:, k:k + 1] * jnp.dot(unsort, ys, preferred_element_type=F32)
        y = yk if y is None else y + yk
    y_ref[...] = y

    def body(i, carry):
        rows = pl.ds(pl.multiple_of(i * ROW_CHUNK, ROW_CHUNK), ROW_CHUNK)
        z = DEEPNORM_ALPHA * x_ref[rows, :] + y_ref[rows, :]
        out_ref[rows, :] = _layer_norm(z, lg_ref[...], lb_ref[...])
        return carry
    lax.fori_loop(0, t // ROW_CHUNK, body, 0, unroll=4)


def _combine(tstart, tcnt, pstart, x2d, ys, lpos_col, gate_col, lg, lb):
    t = SORT_TILE
    grid_spec = pltpu.PrefetchScalarGridSpec(
        num_scalar_prefetch=3,
        grid=(N_SORT_TILES,),
        in_specs=[pl.BlockSpec((t, D_MODEL), lambda i, *_: (i, 0)),
                  pl.BlockSpec(memory_space=pl.ANY),
                  pl.BlockSpec((t, TOP_K), lambda i, *_: (i, 0)),
                  pl.BlockSpec((t, TOP_K), lambda i, *_: (i, 0)),
                  pl.BlockSpec((1, D_MODEL), lambda i, *_: (0, 0)),
                  pl.BlockSpec((1, D_MODEL), lambda i, *_: (0, 0))],
        out_specs=pl.BlockSpec((t, D_MODEL), lambda i, *_: (i, 0)),
        scratch_shapes=[pltpu.VMEM((2, SORT_SLOTS, D_PACK), U32),
                        pltpu.VMEM((t, D_MODEL), F32),
                        pltpu.SemaphoreType.DMA((2,))],
    )
    return pl.pallas_call(
        _combine_kernel,
        out_shape=jax.ShapeDtypeStruct((N_TOK, D_MODEL), F32),
        grid_spec=grid_spec,
        compiler_params=pltpu.CompilerParams(
            dimension_semantics=("arbitrary",), vmem_limit_bytes=VMEM_LIMIT),
        name="moe_combine",
    )(tstart, tcnt, pstart, x2d, ys, lpos_col, gate_col, lg, lb)


def _moe(layer, x1, xsort, lpos, gates, tstart, tcnt, tot, w_gate, w_up, w_down, lg, lb):
    tot = tot[:, 0].astype(I32)
    region = ((tot + EXPERT_BLOCK - 1) // EXPERT_BLOCK) * EXPERT_BLOCK
    pend = jnp.cumsum(region).astype(I32)
    pstart = pend - region
    n_live = (pend[N_EXPERTS - 1:] // EXPERT_BLOCK).astype(I32)
    blk_start = jnp.arange(N_BLOCKS, dtype=I32) * EXPERT_BLOCK
    blk_expert = jnp.minimum(jnp.sum(blk_start[:, None] >= pend[None, :], axis=1),
                             N_EXPERTS - 1).astype(I32)
    tstart_flat = tstart[:, :, 0].astype(I32).reshape(-1)
    tcnt_flat = tcnt[:, :, 0].astype(I32).reshape(-1)

    xs = _dispatch(tstart_flat, tcnt_flat, pstart, pstart + tot, pend, xsort)
    ys = _experts(layer, blk_expert, n_live, xs, w_gate, w_up, w_down)
    out = _combine(tstart_flat, tcnt_flat, pstart, x1.reshape(N_TOK, D_MODEL), ys, lpos.T, gates.T,
                   lg, lb)
    return out.reshape(BATCH, SEQ, D_MODEL)


def kernel(x, positions, conv_w1, conv_b1, conv_dw, conv_dwb, conv_ln_g, conv_ln_b, conv_w2,
           conv_b2, kv_w, kv_b, attn_wq, attn_bq, attn_sinks, attn_wo, attn_bo, router_w,
           router_b, moe_w_gate, moe_w_up, moe_w_down, ln_g, ln_b):
    row = lambda v: v.reshape(1, -1)
    rwt = router_w.T.astype(BF16)
    rb = router_b.astype(F32).reshape(N_EXPERTS, 1)
    idx = jnp.arange(SORT_TILE, dtype=I32)
    tri = (idx[:, None] < idx[None, :]).astype(BF16)
    eidx = jnp.arange(N_EXPERTS, dtype=I32)
    ltri = (eidx[None, :] < eidx[:, None]).astype(BF16)
    route_consts = (rwt, rb, tri, ltri)
    inv_freq = ROPE_THETA ** (-jnp.arange(0, HEAD_DIM, 2, dtype=F32) / HEAD_DIM)
    invf = jnp.tile(inv_freq, LANES // (HEAD_DIM // 2)).reshape(1, LANES)
    pos_col = positions.reshape(N_TOK, 1)

    k = v = cos = sin = None
    for layer in range(DEPTH):
        lg0, lb0 = row(ln_g[layer, 0]), row(ln_b[layer, 0])
        if layer < N_CONV_LAYERS:
            i = layer
            dw3 = conv_dw[i].reshape(CONV_WIDTH, N_LANE_BLOCKS, LANES).transpose(1, 0, 2)
            outs = _conv_layer(
                x, conv_w1[i].astype(BF16), row(conv_b1[i]), dw3, row(conv_dwb[i]),
                row(conv_ln_g[i]), row(conv_ln_b[i]), conv_w2[i].astype(BF16), row(conv_b2[i]),
                lg0, lb0, route_consts)
        else:
            if layer == N_CONV_LAYERS:
                k, v, cos, sin = _shared_kv(x.reshape(N_TOK, D_MODEL), pos_col, invf,
                                            kv_w.astype(BF16), row(kv_b))
            j = layer - N_CONV_LAYERS
            outs = _attn_layer(
                x, k, v, cos, sin, attn_wq[j].astype(BF16), row(attn_bq[j]),
                attn_sinks[j].astype(F32), attn_wo[j].astype(BF16), row(attn_bo[j]),
                lg0, lb0, route_consts)
        x = _moe(layer, *outs, moe_w_gate, moe_w_up, moe_w_down,
                 row(ln_g[layer, 1]), row(ln_b[layer, 1]))
    return x
```
